```python
import math
import jax
import jax.numpy as jnp
from jax import lax
import numpy as np

D_MODEL = 1024
BATCH = 8
SEQ = 2048
DEPTH = 4

N_MIXERS = 3
N_SSD_LAYERS = (DEPTH + 2) // N_MIXERS
N_LRU_LAYERS = (DEPTH + 1) // N_MIXERS
N_ATTN_LAYERS = DEPTH // N_MIXERS
NORM_EPS = 1e-6
D_FF = 4 * D_MODEL
N_MOD = 6

SSD_D_INNER = 2 * D_MODEL
SSD_HEAD_DIM = 64
SSD_N_HEADS = SSD_D_INNER // SSD_HEAD_DIM
SSD_N_GROUPS = 8
SSD_HEADS_PER_GROUP = SSD_N_HEADS // SSD_N_GROUPS
SSD_D_STATE = 128
SSD_CONV_WIDTH = 4
SSD_CHUNK = 128
SSD_CONV_DIM = SSD_D_INNER + 2 * SSD_N_GROUPS * SSD_D_STATE
SSD_IN_DIM = SSD_D_INNER + SSD_CONV_DIM + SSD_N_HEADS

LRU_WIDTH = D_MODEL
LRU_N_BLOCKS = 4
LRU_BLOCK = LRU_WIDTH // LRU_N_BLOCKS
LRU_CONV_WIDTH = 4
LRU_C = 8.0

ATTN_HEAD_DIM = 64
ATTN_HEADS_PER_GROUP = 8
ATTN_GROUP_WIDTH = ATTN_HEADS_PER_GROUP * ATTN_HEAD_DIM
ATTN_CONFIGS = ((128, 1), (512, 4), (2048, 16))
ATTN_N_GROUPS = len(ATTN_CONFIGS)
ATTN_QKV_DIM = ATTN_N_GROUPS * 3 * ATTN_GROUP_WIDTH
ROPE_THETA = 500000.0
ROPE_DIM = ATTN_HEAD_DIM // 4
MASK_VALUE = -1e30

kernel_name = 'hybrid_ssd_rglru_dilated_attn_trunk'


def rms_norm(x, g):
    xf = x.astype(jnp.float32)
    xf = xf * lax.rsqrt(jnp.mean(jnp.square(xf), axis=-1, keepdims=True) + NORM_EPS)
    return xf.astype(x.dtype) * g


def grouped_rms_norm(x, g, n_groups):
    b, s, d = x.shape
    xg = x.astype(jnp.float32).reshape(b, s, n_groups, d // n_groups)
    xg = xg * lax.rsqrt(jnp.mean(jnp.square(xg), axis=-1, keepdims=True) + NORM_EPS)
    return xg.reshape(b, s, d).astype(x.dtype) * g


def causal_dwconv(x, w, bias):
    k = w.shape[0]
    y = lax.conv_general_dilated(x, w[:, None, :], window_strides=(1,), padding=[(k - 1, 0)],
                                 dimension_numbers=('NWC', 'WIO', 'NWC'),
                                 feature_group_count=x.shape[-1])
    return y + bias


def rope_tables(positions, dtype):
    inv_freq = ROPE_THETA ** (-jnp.arange(0, ROPE_DIM, 2, dtype=jnp.float32) / ROPE_DIM)
    ang = positions.astype(jnp.float32)[..., None] * inv_freq
    return jnp.cos(ang)[:, :, None, :].astype(dtype), jnp.sin(ang)[:, :, None, :].astype(dtype)


def apply_partial_rope(x, cos, sin):
    xr, xp = x[..., :ROPE_DIM], x[..., ROPE_DIM:]
    x1, x2 = jnp.split(xr, 2, axis=-1)
    rot = jnp.concatenate([x1 * cos - x2 * sin, x2 * cos + x1 * sin], axis=-1)
    return jnp.concatenate([rot, xp], axis=-1)


def ssd_mixer(h, w_in, conv_w, conv_b, dt_bias, a_log, d_skip, norm_g, w_out):
    b, s, _ = h.shape
    f32 = jnp.float32
    nc, L = s // SSD_CHUNK, SSD_CHUNK
    G, E, P, N = SSD_N_GROUPS, SSD_HEADS_PER_GROUP, SSD_HEAD_DIM, SSD_D_STATE
    proj = h @ w_in
    z, xbc, dt = jnp.split(proj, [SSD_D_INNER, SSD_D_INNER + SSD_CONV_DIM], axis=-1)
    xbc = jax.nn.silu(causal_dwconv(xbc, conv_w, conv_b))
    xs, bm, cm = jnp.split(xbc, [SSD_D_INNER, SSD_D_INNER + G * N], axis=-1)
    dt = jax.nn.softplus(dt.astype(f32) + dt_bias.astype(f32))
    a = -jnp.exp(a_log.astype(f32)).reshape(G, E)
    x_c = xs.astype(f32).reshape(b, nc, L, G, E, P)
    b_c = bm.astype(f32).reshape(b, nc, L, G, N)
    c_c = cm.astype(f32).reshape(b, nc, L, G, N)
    dt_c = dt.reshape(b, nc, L, G, E)
    a_cs = jnp.cumsum(dt_c * a, axis=2)
    xdt = x_c * dt_c[..., None]
    tri = jnp.tril(jnp.ones((L, L), dtype=bool))[:, :, None, None]
    seg = a_cs[:, :, :, None] - a_cs[:, :, None, :]
    decay = jnp.exp(jnp.where(tri, seg, -jnp.inf))
    cb = jnp.einsum('bclgn,bcsgn->bclsg', c_c, b_c)
    y_diag = jnp.einsum('bclsg,bclsge,bcsgep->bclgep', cb, decay, xdt)
    decay_to_end = jnp.exp(a_cs[:, :, -1:] - a_cs)
    chunk_states = jnp.einsum('bclgn,bclge,bclgep->bcgepn', b_c, decay_to_end, xdt)
    chunk_decay = jnp.exp(a_cs[:, :, -1])

    def step(state, inp):
        dec, new = inp
        return state * dec[..., None, None] + new, state

    init = jnp.zeros((b, G, E, P, N), f32)
    _, prev_states = lax.scan(step, init, (jnp.moveaxis(chunk_decay, 1, 0),
                                           jnp.moveaxis(chunk_states, 1, 0)))
    prev_states = jnp.moveaxis(prev_states, 0, 1)
    y_off = jnp.einsum('bclgn,bcgepn,bclge->bclgep', c_c, prev_states, jnp.exp(a_cs))
    y = y_diag + y_off + x_c * d_skip.astype(f32).reshape(G, E)[..., None]
    y = y.reshape(b, s, SSD_D_INNER) * jax.nn.silu(z.astype(f32))
    y = grouped_rms_norm(y, norm_g.astype(f32), G).astype(h.dtype)
    return y @ w_out


def rglru_mixer(h, w_in, conv_w, conv_b, w_ga, b_ga, w_gx, b_gx, lam, w_out):
    b, s, _ = h.shape
    f32 = jnp.float32
    gate_branch, xr = jnp.split(h @ w_in, 2, axis=-1)
    xr = causal_dwconv(xr, conv_w, conv_b)
    xb = xr.reshape(b, s, LRU_N_BLOCKS, LRU_BLOCK)
    r = jax.nn.sigmoid(jnp.einsum('bshi,hij->bshj', xb, w_ga) + b_ga.reshape(LRU_N_BLOCKS, LRU_BLOCK))
    i = jax.nn.sigmoid(jnp.einsum('bshi,hij->bshj', xb, w_gx) + b_gx.reshape(LRU_N_BLOCKS, LRU_BLOCK))
    r = r.reshape(b, s, LRU_WIDTH).astype(f32)
    i = i.reshape(b, s, LRU_WIDTH).astype(f32)
    log_a = -LRU_C * r * jax.nn.softplus(-lam.astype(f32))
    a = jnp.exp(log_a)
    u = jnp.sqrt(-jnp.expm1(2.0 * log_a)) * (i * xr.astype(f32))

    def combine(left, right):
        a_l, b_l = left
        a_r, b_r = right
        return a_l * a_r, a_r * b_l + b_r

    _, hs = lax.associative_scan(combine, (a, u), axis=1)
    y = hs.astype(h.dtype) * jax.nn.gelu(gate_branch, approximate=True)
    return y @ w_out


def dilated_group_attention(q, k, v, dilation, span):
    b, s, hh, dh = q.shape
    m = s // dilation
    nblk = -(-m // span)
    mp = nblk * span

    def to_sub(t):
        t = t.reshape(b, m, dilation, hh, dh)
        t = jnp.pad(t, ((0, 0), (0, mp - m), (0, 0), (0, 0), (0, 0)))
        return t.reshape(b, nblk, span, dilation, hh, dh)

    qs, ks, vs = to_sub(q), to_sub(k), to_sub(v)
    pad = ((0, 0), (1, 0), (0, 0), (0, 0), (0, 0), (0, 0))
    kp, vp = jnp.pad(ks, pad), jnp.pad(vs, pad)
    k_win = jnp.concatenate([kp[:, :-1], kp[:, 1:]], axis=2)
    v_win = jnp.concatenate([vp[:, :-1], vp[:, 1:]], axis=2)
    scores = jnp.einsum('bnqrhd,bnkrhd->bnrhqk', qs, k_win).astype(jnp.float32)
    qi = jnp.arange(span)[:, None]
    ki = jnp.arange(2 * span)[None, :]
    dist = qi + span - ki
    key_pos = jnp.arange(nblk)[:, None, None] * span - span + ki[None]
    valid = (dist >= 0) & (dist <= span) & (key_pos >= 0)
    scores = jnp.where(valid[None, :, None, None], scores, MASK_VALUE)
    lse = jax.nn.logsumexp(scores, axis=-1)
    probs = jnp.exp(scores - lse[..., None]).astype(v.dtype)
    out = jnp.einsum('bnrhqk,bnkrhd->bnqrhd', probs, v_win)
    out = out.reshape(b, mp * dilation, hh, dh)[:, :s]
    lse = jnp.transpose(lse, (0, 1, 4, 2, 3)).reshape(b, mp * dilation, hh)[:, :s]
    return out, lse


def dilated_attention_mixer(h, cos, sin, w_qkv, w_out):
    b, s, _ = h.shape
    scale = ATTN_HEAD_DIM ** -0.5
    qkv = (h @ w_qkv).reshape(b, s, ATTN_N_GROUPS, 3, ATTN_HEADS_PER_GROUP, ATTN_HEAD_DIM)
    outs, lses = [], []
    for g, (window, dilation) in enumerate(ATTN_CONFIGS):
        q = apply_partial_rope(qkv[:, :, g, 0], cos, sin) * scale
        k = apply_partial_rope(qkv[:, :, g, 1], cos, sin)
        v = qkv[:, :, g, 2]
        o, lse = dilated_group_attention(q, k, v, dilation, window // dilation)
        outs.append(o.astype(jnp.float32))
        lses.append(lse)
    w = jax.nn.softmax(jnp.stack(lses, axis=0), axis=0)
    o = jnp.sum(w[..., None] * jnp.stack(outs, axis=0), axis=0)
    return o.reshape(b, s, ATTN_GROUP_WIDTH).astype(h.dtype) @ w_out


def _normal(key, shape, fan_in):
    return jax.random.normal(key, shape, jnp.float32) * fan_in ** -0.5


def _gain(key, shape):
    return 1.0 + 0.05 * jax.random.normal(key, shape, jnp.float32)


def _small(key, shape):
    return 0.02 * jax.random.normal(key, shape, jnp.float32)


def setup_inputs(seed: int = 0) -> dict:
    key = jax.random.key(seed)
    ks = jax.random.split(key, 32)
    nA, nB, nC = N_SSD_LAYERS, N_LRU_LAYERS, N_ATTN_LAYERS
    x = jax.random.normal(ks[0], (BATCH, SEQ, D_MODEL), jnp.float32)
    c = jax.random.normal(ks[1], (BATCH, D_MODEL), jnp.float32)
    offsets = jax.random.randint(ks[2], (BATCH, 1), 0, 1024, dtype=jnp.int32)
    positions = offsets + jnp.arange(SEQ, dtype=jnp.int32)[None, :]
    ada_w = 0.2 * _normal(ks[3], (DEPTH, D_MODEL, N_MOD * D_MODEL), D_MODEL)
    ada_b = _small(ks[4], (DEPTH, N_MOD * D_MODEL))
    norm_mix_pre = _gain(ks[5], (DEPTH, D_MODEL))
    norm_mix_post = _gain(ks[6], (DEPTH, D_MODEL))
    norm_mlp_pre = _gain(ks[7], (DEPTH, D_MODEL))
    norm_mlp_post = _gain(ks[8], (DEPTH, D_MODEL))
    mlp_w1 = _normal(ks[9], (DEPTH, D_MODEL, D_FF), D_MODEL)
    mlp_w2 = _normal(ks[10], (DEPTH, D_FF, D_MODEL), D_FF)
    ssd_w_in = _normal(ks[11], (nA, D_MODEL, SSD_IN_DIM), D_MODEL)
    ssd_conv_w = _normal(ks[12], (nA, SSD_CONV_WIDTH, SSD_CONV_DIM), SSD_CONV_WIDTH)
    ssd_conv_b = _small(ks[13], (nA, SSD_CONV_DIM))
    dt0 = jnp.exp(jax.random.uniform(ks[14], (nA, SSD_N_HEADS), jnp.float32,
                                     minval=math.log(1e-3), maxval=math.log(1e-1)))
    ssd_dt_bias = dt0 + jnp.log(-jnp.expm1(-dt0))
    ssd_a_log = jnp.log(jax.random.uniform(ks[15], (nA, SSD_N_HEADS), jnp.float32, minval=1.0, maxval=16.0))
    ssd_d = _gain(ks[16], (nA, SSD_N_HEADS))
    ssd_norm = _gain(ks[17], (nA, SSD_D_INNER))
    ssd_w_out = _normal(ks[18], (nA, SSD_D_INNER, D_MODEL), SSD_D_INNER)
    lru_w_in = _normal(ks[19], (nB, D_MODEL, 2 * LRU_WIDTH), D_MODEL)
    lru_conv_w = _normal(ks[20], (nB, LRU_CONV_WIDTH, LRU_WIDTH), LRU_CONV_WIDTH)
    lru_conv_b = _small(ks[21], (nB, LRU_WIDTH))
    lru_w_gate_a = _normal(ks[22], (nB, LRU_N_BLOCKS, LRU_BLOCK, LRU_BLOCK), LRU_BLOCK)
    lru_b_gate_a = _small(ks[23], (nB, LRU_WIDTH))
    lru_w_gate_x = _normal(ks[24], (nB, LRU_N_BLOCKS, LRU_BLOCK, LRU_BLOCK), LRU_BLOCK)
    lru_b_gate_x = _small(ks[25], (nB, LRU_WIDTH))
    a_target = jax.random.uniform(ks[26], (nB, LRU_WIDTH), jnp.float32, minval=0.9, maxval=0.999)
    p = a_target ** (1.0 / LRU_C)
    lru_lambda = jnp.log(p) - jnp.log1p(-p)
    lru_w_out = _normal(ks[27], (nB, LRU_WIDTH, D_MODEL), LRU_WIDTH)
    attn_w_qkv = _normal(ks[28], (nC, D_MODEL, ATTN_QKV_DIM), D_MODEL)
    attn_w_out = _normal(ks[29], (nC, ATTN_GROUP_WIDTH, D_MODEL), ATTN_GROUP_WIDTH)
    return {'x': x, 'c': c, 'positions': positions, 'ada_w': ada_w, 'ada_b': ada_b,
            'norm_mix_pre': norm_mix_pre, 'norm_mix_post': norm_mix_post,
            'norm_mlp_pre': norm_mlp_pre, 'norm_mlp_post': norm_mlp_post,
            'mlp_w1': mlp_w1, 'mlp_w2': mlp_w2,
            'ssd_w_in': ssd_w_in, 'ssd_conv_w': ssd_conv_w, 'ssd_conv_b': ssd_conv_b,
            'ssd_dt_bias': ssd_dt_bias, 'ssd_a_log': ssd_a_log, 'ssd_d': ssd_d,
            'ssd_norm': ssd_norm, 'ssd_w_out': ssd_w_out,
            'lru_w_in': lru_w_in, 'lru_conv_w': lru_conv_w, 'lru_conv_b': lru_conv_b,
            'lru_w_gate_a': lru_w_gate_a, 'lru_b_gate_a': lru_b_gate_a,
            'lru_w_gate_x': lru_w_gate_x, 'lru_b_gate_x': lru_b_gate_x,
            'lru_lambda': lru_lambda, 'lru_w_out': lru_w_out,
            'attn_w_qkv': attn_w_qkv, 'attn_w_out': attn_w_out}


def reference(x, c, positions, ada_w, ada_b, norm_mix_pre, norm_mix_post, norm_mlp_pre, norm_mlp_post,
              mlp_w1, mlp_w2, ssd_w_in, ssd_conv_w, ssd_conv_b, ssd_dt_bias, ssd_a_log, ssd_d, ssd_norm,
              ssd_w_out, lru_w_in, lru_conv_w, lru_conv_b, lru_w_gate_a, lru_b_gate_a, lru_w_gate_x,
              lru_b_gate_x, lru_lambda, lru_w_out, attn_w_qkv, attn_w_out):
    cos, sin = rope_tables(positions, x.dtype)
    mod = jnp.einsum('bd,lde->lbe', jax.nn.silu(c), ada_w) + ada_b[:, None, :]
    for layer in range(DEPTH):
        sh_m, sc_m, g_m, sh_f, sc_f, g_f = jnp.split(mod[layer][:, None, :], N_MOD, axis=-1)
        h = rms_norm(x, norm_mix_pre[layer]) * (1.0 + sc_m) + sh_m
        kind, occ = layer % N_MIXERS, layer // N_MIXERS
        if kind == 0:
            y = ssd_mixer(h, ssd_w_in[occ], ssd_conv_w[occ], ssd_conv_b[occ], ssd_dt_bias[occ],
                          ssd_a_log[occ], ssd_d[occ], ssd_norm[occ], ssd_w_out[occ])
        elif kind == 1:
            y = rglru_mixer(h, lru_w_in[occ], lru_conv_w[occ], lru_conv_b[occ], lru_w_gate_a[occ],
                            lru_b_gate_a[occ], lru_w_gate_x[occ], lru_b_gate_x[occ], lru_lambda[occ],
                            lru_w_out[occ])
        else:
            y = dilated_attention_mixer(h, cos, sin, attn_w_qkv[occ], attn_w_out[occ])
        x = x + (1.0 + g_m) * rms_norm(y, norm_mix_post[layer])
        h = rms_norm(x, norm_mlp_pre[layer]) * (1.0 + sc_f) + sh_f
        y = jnp.square(jax.nn.relu(h @ mlp_w1[layer])) @ mlp_w2[layer]
        x = x + (1.0 + g_f) * rms_norm(y, norm_mlp_post[layer])
    return x
```

```python
import functools

import jax
import jax.numpy as jnp
from jax import lax
from jax.experimental import pallas as pl
from jax.experimental.pallas import tpu as pltpu

F32 = jnp.float32
BF16 = jnp.bfloat16

NORM_EPS = 1e-6
N_MIXERS = 3
N_MOD = 6

SSD_HEAD_DIM = 64
SSD_N_GROUPS = 8
SSD_D_STATE = 128
SSD_CHUNK = 128

LRU_N_BLOCKS = 4
LRU_C = 8.0
LRU_CHUNK = 256

ATTN_HEAD_DIM = 64
ATTN_HEADS = 8
ATTN_CONFIGS = ((128, 1), (512, 4), (2048, 16))
ROPE_THETA = 500000.0
ROPE_DIM = ATTN_HEAD_DIM // 4
MASK_VALUE = -1e30

LANES = 128
SUBLANES = 8
VMEM_LIMIT = 56 * 1024 * 1024

ROW_TILE = 512
COL_TILE = 512


def _cparams(*sem):
    return pltpu.CompilerParams(dimension_semantics=sem, vmem_limit_bytes=VMEM_LIMIT)


def _dot(a, b):
    return jnp.dot(a, b, preferred_element_type=F32)


def _dot_nt(a, b):
    return lax.dot_general(a, b, (((1,), (1,)), ((), ())), preferred_element_type=F32)


def _split3(v):
    hi = v.astype(BF16)
    r1 = v - hi.astype(F32)
    mid = r1.astype(BF16)
    lo = (r1 - mid.astype(F32)).astype(BF16)
    return hi, mid, lo


def _rms(y):
    return y * lax.rsqrt(jnp.mean(y * y, axis=-1, keepdims=True) + NORM_EPS)


def _norm_mod(x, g, sc, sh):
    return _rms(x) * g * (1.0 + sc) + sh


def _residual(x, y, gate, g_post):
    return x + (1.0 + gate) * (_rms(y) * g_post)


def _softplus(x):
    return jnp.maximum(x, 0.0) + jnp.log1p(jnp.exp(-jnp.abs(x)))


def _silu(x):
    return x * jax.nn.sigmoid(x)


def _causal_conv(x, tail, w, b):
    row8 = lax.broadcasted_iota(jnp.int32, (SUBLANES, x.shape[1]), 0)
    acc = x * w[3:4, :] + b
    for j in (1, 2, 3):
        sh = pltpu.roll(x, j, 0)
        top = jnp.where(row8 < j, pltpu.roll(tail, j, 0), sh[0:SUBLANES])
        sh = jnp.concatenate([top, sh[SUBLANES:]], axis=0)
        acc = acc + sh * w[3 - j:4 - j, :]
    return acc


def _mod_kernel(c_ref, w_ref, b_ref, o_ref):
    c = c_ref[...]
    a_hi, a_mid, _ = _split3(_silu(c))
    w = w_ref[...]
    w_hi = w.astype(BF16)
    w_lo = (w - w_hi.astype(F32)).astype(BF16)
    o_ref[...] = _dot(a_hi, w_hi) + _dot(a_hi, w_lo) + _dot(a_mid, w_hi) + b_ref[...]


def _modulation(c, ada_w, ada_b):
    depth, d, e = ada_w.shape
    b = c.shape[0]
    tn = 1024
    return pl.pallas_call(
        _mod_kernel,
        grid=(depth, e // tn),
        in_specs=[pl.BlockSpec((b, d), lambda l, j: (0, 0)),
                  pl.BlockSpec((None, d, tn), lambda l, j: (l, 0, j)),
                  pl.BlockSpec((None, 1, tn), lambda l, j: (l, 0, j))],
        out_specs=pl.BlockSpec((None, b, tn), lambda l, j: (l, 0, j)),
        out_shape=jax.ShapeDtypeStruct((depth, b, e), F32),
        compiler_params=_cparams("parallel", "parallel"),
        name="adaln_mod",
    )(c, ada_w, ada_b.reshape(depth, 1, e))


def _rope(y, rc, rs1, rs2):
    reps = y.shape[1] // LANES
    tile = lambda t: jnp.concatenate([t] * reps, axis=1)
    n = y.shape[1]
    half = ROPE_DIM // 2
    return y * tile(rc) + pltpu.roll(y, n - half, 1) * tile(rs1) + pltpu.roll(y, half, 1) * tile(rs2)


def _inproj_kernel(*refs, has_dt, has_rope, q_scale):
    x_ref, g_ref, sc_ref, sh_ref, w_ref = refs[:5]
    pos = 5
    if has_dt:
        wdt_ref = refs[pos]
        pos += 1
    if has_rope:
        rc_ref, rs1_ref, rs2_ref = refs[pos:pos + 3]
        pos += 3
    o_ref = refs[pos]
    pos += 1
    if has_dt:
        odt_ref = refs[pos]
        pos += 1
    h_scr = refs[pos]
    j = pl.program_id(1)

    @pl.when(j == 0)
    def _():
        h_scr[...] = _norm_mod(x_ref[...], g_ref[...], sc_ref[...], sh_ref[...]).astype(BF16)
        if has_dt:
            odt_ref[...] = _dot(h_scr[...], wdt_ref[...])

    acc = _dot(h_scr[...], w_ref[...])
    if has_rope:
        kind = j % 3

        @pl.when(kind == 2)
        def _():
            o_ref[...] = acc.astype(o_ref.dtype)

        @pl.when(kind < 2)
        def _():
            r = _rope(acc, rc_ref[...], rs1_ref[...], rs2_ref[...])
            r = r * jnp.where(kind == 0, q_scale, 1.0)
            o_ref[...] = r.astype(o_ref.dtype)
    else:
        o_ref[...] = acc.astype(o_ref.dtype)


def _inproj(x, seq, g, modl, which, w, w_dt=None, rope=None, q_scale=1.0):
    t, d = x.shape
    n = w.shape[1]
    tm, tn = ROW_TILE, COL_TILE
    per = seq // tm
    sc_map = lambda i, j: ((i // per) * N_MOD + which + 1, 0, 0)
    sh_map = lambda i, j: ((i // per) * N_MOD + which, 0, 0)
    in_specs = [pl.BlockSpec((tm, d), lambda i, j: (i, 0)),
                pl.BlockSpec((1, d), lambda i, j: (0, 0)),
                pl.BlockSpec((None, 1, d), sc_map),
                pl.BlockSpec((None, 1, d), sh_map),
                pl.BlockSpec((d, tn), lambda i, j: (0, j))]
    args = [x, g.reshape(1, d), modl, modl, w]
    out_specs = [pl.BlockSpec((tm, tn), lambda i, j: (i, j))]
    out_shape = [jax.ShapeDtypeStruct((t, n), BF16)]
    if w_dt is not None:
        in_specs.append(pl.BlockSpec((d, LANES), lambda i, j: (0, 0)))
        args.append(w_dt)
        out_specs.append(pl.BlockSpec((tm, LANES), lambda i, j: (i, 0)))
        out_shape.append(jax.ShapeDtypeStruct((t, LANES), F32))
    if rope is not None:
        for tab in rope:
            in_specs.append(pl.BlockSpec((tm, LANES), lambda i, j: (i, 0)))
            args.append(tab)
    res = pl.pallas_call(
        functools.partial(_inproj_kernel, has_dt=w_dt is not None, has_rope=rope is not None,
                          q_scale=q_scale),
        grid=(t // tm, n // tn),
        in_specs=in_specs,
        out_specs=out_specs,
        out_shape=out_shape,
        scratch_shapes=[pltpu.VMEM((tm, d), BF16)],
        compiler_params=_cparams("parallel", "arbitrary"),
        name="inproj",
    )(*args)
    return res if w_dt is not None else res[0]


def _outproj_kernel(y_ref, w_ref, x_ref, gate_ref, gp_ref, o_ref, acc_scr):
    k = pl.program_id(1)

    @pl.when(k == 0)
    def _():
        acc_scr[...] = jnp.zeros_like(acc_scr)

    acc_scr[...] += _dot(y_ref[...], w_ref[...])

    @pl.when(k == pl.num_programs(1) - 1)
    def _():
        o_ref[...] = _residual(x_ref[...], acc_scr[...], gate_ref[...], gp_ref[...])


def _outproj(y, w, x, seq, modl, which, g_post):
    t, kdim = y.shape
    d = w.shape[1]
    tm, tk = ROW_TILE, COL_TILE
    per = seq // tm
    return pl.pallas_call(
        _outproj_kernel,
        grid=(t // tm, kdim // tk),
        in_specs=[pl.BlockSpec((tm, tk), lambda i, k: (i, k)),
                  pl.BlockSpec((tk, d), lambda i, k: (k, 0)),
                  pl.BlockSpec((tm, d), lambda i, k: (i, 0)),
                  pl.BlockSpec((None, 1, d), lambda i, k: ((i // per) * N_MOD + which, 0, 0)),
                  pl.BlockSpec((1, d), lambda i, k: (0, 0))],
        out_specs=pl.BlockSpec((tm, d), lambda i, k: (i, 0)),
        out_shape=jax.ShapeDtypeStruct((t, d), F32),
        scratch_shapes=[pltpu.VMEM((tm, d), F32)],
        compiler_params=_cparams("parallel", "arbitrary"),
        name="outproj",
    )(y, w, x, modl, g_post.reshape(1, d))


def _mlp_kernel(x_ref, g_ref, sc_ref, sh_ref, w1_ref, w2_ref, gate_ref, gp_ref, o_ref, h_scr, acc_scr):
    f = pl.program_id(1)

    @pl.when(f == 0)
    def _():
        h_scr[...] = _norm_mod(x_ref[...], g_ref[...], sc_ref[...], sh_ref[...]).astype(BF16)
        acc_scr[...] = jnp.zeros_like(acc_scr)

    a = jnp.maximum(_dot(h_scr[...], w1_ref[...]), 0.0)
    acc_scr[...] += _dot((a * a).astype(BF16), w2_ref[...])

    @pl.when(f == pl.num_programs(1) - 1)
    def _():
        o_ref[...] = _residual(x_ref[...], acc_scr[...], gate_ref[...], gp_ref[...])


def _mlp(x, seq, modl, g_pre, g_post, w1, w2):
    t, d = x.shape
    dff = w1.shape[1]
    tm, tf = ROW_TILE, COL_TILE
    per = seq // tm
    mod_spec = lambda which: pl.BlockSpec((None, 1, d), lambda i, f: ((i // per) * N_MOD + which, 0, 0))
    return pl.pallas_call(
        _mlp_kernel,
        grid=(t // tm, dff // tf),
        in_specs=[pl.BlockSpec((tm, d), lambda i, f: (i, 0)),
                  pl.BlockSpec((1, d), lambda i, f: (0, 0)),
                  mod_spec(4), mod_spec(3),
                  pl.BlockSpec((d, tf), lambda i, f: (0, f)),
                  pl.BlockSpec((tf, d), lambda i, f: (f, 0)),
                  mod_spec(5),
                  pl.BlockSpec((1, d), lambda i, f: (0, 0))],
        out_specs=pl.BlockSpec((tm, d), lambda i, f: (i, 0)),
        out_shape=jax.ShapeDtypeStruct((t, d), F32),
        scratch_shapes=[pltpu.VMEM((tm, d), BF16), pltpu.VMEM((tm, d), F32)],
        compiler_params=_cparams("parallel", "arbitrary"),
        name="mlp",
    )(x, g_pre.reshape(1, d), modl, modl, w1, w2, modl, g_post.reshape(1, d))


def _ssd_kernel(zx_ref, dt_ref, cw_ref, cb_ref, dtb_ref, alog_ref, dsk_ref, ng_ref, exp_ref,
                y_ref, state_scr, tail_scr, act_scr, *, d_inner):
    L, N, G = SSD_CHUNK, SSD_D_STATE, SSD_N_GROUPS
    gw = d_inner // G
    conv_dim = d_inner + 2 * G * N
    c = pl.program_id(1)

    @pl.when(c == 0)
    def _():
        state_scr[...] = jnp.zeros_like(state_scr)
        tail_scr[...] = jnp.zeros_like(tail_scr)

    strip = 512
    for s0 in range(0, conv_dim, strip):
        cols = slice(s0, s0 + strip)
        xin = zx_ref[:, d_inner + s0:d_inner + s0 + strip].astype(F32)
        y = _causal_conv(xin, tail_scr[:, cols], cw_ref[:, cols], cb_ref[:, cols])
        tail_scr[:, cols] = xin[L - SUBLANES:L]
        act_scr[:, cols] = _silu(y)

    dt = _softplus(dt_ref[...] + dtb_ref[...])
    a = -jnp.exp(alog_ref[...])
    row = lax.broadcasted_iota(jnp.int32, (L, L), 0)
    col = lax.broadcasted_iota(jnp.int32, (L, L), 1)
    tri = row >= col
    tril = jnp.where(tri, 1.0, 0.0).astype(BF16)
    a_cs = sum(_dot(tril, term) for term in _split3(dt * a))
    a_cs_t = a_cs.T
    dt_t = dt.T
    e_cs = jnp.exp(a_cs)
    dte = jnp.exp(a_cs[L - 1:L, :] - a_cs) * dt
    expand = exp_ref[...]
    hi, mid, _ = _split3(e_cs)
    e_cs_x = _dot(hi, expand) + _dot(mid, expand)
    hi, mid, _ = _split3(dte)
    dte_x = _dot(hi, expand) + _dot(mid, expand)

    lane = lax.broadcasted_iota(jnp.int32, (L, LANES), 1)
    first_head = lane < SSD_HEAD_DIM
    heads_per_pair = LANES // SSD_HEAD_DIM
    for g in range(G):
        gs = slice(g * gw, (g + 1) * gw)
        xs_g = act_scr[:, gs]
        bm_g = act_scr[:, d_inner + g * N:d_inner + (g + 1) * N]
        cm_g = act_scr[:, d_inner + G * N + g * N:d_inner + G * N + (g + 1) * N].astype(BF16)
        cb = _dot_nt(cm_g, bm_g.astype(BF16))
        st_prev = state_scr[g]
        y_off = _dot(cm_g, st_prev.astype(BF16)) * e_cs_x[:, gs]
        w_g = (xs_g * dte_x[:, gs]).astype(BF16)
        new_state = _dot(bm_g.T.astype(BF16), w_g)
        state_scr[g] = st_prev * e_cs_x[L - 1:L, gs] + new_state
        y_parts = []
        for q in range(gw // LANES):
            xs_p = xs_g[:, q * LANES:(q + 1) * LANES]
            ms = []
            for hh in range(heads_per_pair):
                h = (g * gw + q * LANES) // SSD_HEAD_DIM + hh
                seg = a_cs[:, h:h + 1] - a_cs_t[h:h + 1, :]
                decay = jnp.exp(jnp.where(tri, seg, -jnp.inf))
                ms.append((cb * decay * dt_t[h:h + 1, :]).astype(BF16))
            m_cat = jnp.concatenate(ms, axis=1)
            x_cat = jnp.concatenate([jnp.where(first_head, xs_p, 0.0),
                                     jnp.where(first_head, 0.0, xs_p)], axis=0).astype(BF16)
            y_parts.append(_dot(m_cat, x_cat))
        y = jnp.concatenate(y_parts, axis=1) + y_off + xs_g * dsk_ref[:, gs]
        y = y * _silu(zx_ref[:, gs].astype(F32))
        y_ref[:, gs] = (_rms(y) * ng_ref[:, gs]).astype(y_ref.dtype)


def _ssd_core(zx, dt_raw, batch, seq, conv_w, conv_b, dt_bias, a_log, d_skip, norm_g):
    t = zx.shape[0]
    n_heads = dt_bias.shape[0]
    d_inner = n_heads * SSD_HEAD_DIM
    conv_dim = conv_w.shape[1]
    L = SSD_CHUNK
    nc = seq // L
    pad = lambda v: jnp.pad(v.astype(F32), (0, LANES - n_heads)).reshape(1, LANES)
    expand = (jnp.arange(LANES)[:, None] == (jnp.arange(d_inner)[None, :] // SSD_HEAD_DIM)).astype(BF16)
    dsk_x = jnp.repeat(d_skip.astype(F32), SSD_HEAD_DIM).reshape(1, d_inner)
    full = lambda shape: pl.BlockSpec(shape, lambda b, c: (0,) * len(shape))
    return pl.pallas_call(
        functools.partial(_ssd_kernel, d_inner=d_inner),
        grid=(batch, nc),
        in_specs=[pl.BlockSpec((L, zx.shape[1]), lambda b, c: (b * nc + c, 0)),
                  pl.BlockSpec((L, LANES), lambda b, c: (b * nc + c, 0)),
                  full((4, conv_dim)), full((1, conv_dim)), full((1, LANES)), full((1, LANES)),
                  full((1, d_inner)), full((1, d_inner)), full((LANES, d_inner))],
        out_specs=pl.BlockSpec((L, d_inner), lambda b, c: (b * nc + c, 0)),
        out_shape=jax.ShapeDtypeStruct((t, d_inner), BF16),
        scratch_shapes=[pltpu.VMEM((SSD_N_GROUPS, SSD_D_STATE, d_inner // SSD_N_GROUPS), F32),
                        pltpu.VMEM((SUBLANES, conv_dim), F32),
                        pltpu.VMEM((L, conv_dim), F32)],
        compiler_params=_cparams("parallel", "arbitrary"),
        name="ssd_core",
    )(zx, dt_raw, conv_w, conv_b.reshape(1, conv_dim), pad(dt_bias), pad(a_log), dsk_x,
      norm_g.reshape(1, d_inner), expand)


def _lru_kernel(p_ref, cw_ref, cb_ref, wga_ref, bga_ref, wgx_ref, bgx_ref, lam_ref, y_ref,
                tail_scr, h_scr, a_scr, u_scr):
    tc, width = y_ref.shape
    blk = width // LRU_N_BLOCKS
    c = pl.program_id(1)

    @pl.when(c == 0)
    def _():
        tail_scr[...] = jnp.zeros_like(tail_scr)
        h_scr[...] = jnp.zeros_like(h_scr)

    xr = p_ref[:, width:2 * width].astype(F32)
    xc = _causal_conv(xr, tail_scr[...], cw_ref[...], cb_ref[...])
    tail_scr[...] = xr[tc - SUBLANES:tc]
    neg_sp = _softplus(-lam_ref[...])
    for k in range(LRU_N_BLOCKS):
        sl = slice(k * blk, (k + 1) * blk)
        xb = xc[:, sl]
        xb16 = xb.astype(BF16)
        r = jax.nn.sigmoid(_dot(xb16, wga_ref[k]) + bga_ref[:, sl])
        i = jax.nn.sigmoid(_dot(xb16, wgx_ref[k]) + bgx_ref[:, sl])
        log_a = -LRU_C * r * neg_sp[:, sl]
        a_scr[:, sl] = jnp.exp(log_a)
        u_scr[:, sl] = jnp.sqrt(1.0 - jnp.exp(2.0 * log_a)) * (i * xb)

    row = lax.broadcasted_iota(jnp.int32, (SUBLANES, width), 0)

    def body(t, h):
        rows = pl.ds(pl.multiple_of(t * SUBLANES, SUBLANES), SUBLANES)
        a = a_scr[rows, :]
        u = u_scr[rows, :]
        for k in (1, 2, 4):
            keep = row >= k
            a_sh = jnp.where(keep, pltpu.roll(a, k, 0), 1.0)
            u_sh = jnp.where(keep, pltpu.roll(u, k, 0), 0.0)
            u = a * u_sh + u
            a = a * a_sh
        hh = a * h + u
        u_scr[rows, :] = hh
        return jnp.broadcast_to(hh[SUBLANES - 1:SUBLANES, :], hh.shape)

    h_scr[...] = lax.fori_loop(0, tc // SUBLANES, body, h_scr[...])
    gate = p_ref[:, 0:width].astype(F32)
    y_ref[...] = (u_scr[...] * jax.nn.gelu(gate, approximate=True)).astype(y_ref.dtype)


def _lru_core(proj, batch, seq, conv_w, conv_b, w_ga, b_ga, w_gx, b_gx, lam):
    t = proj.shape[0]
    width = conv_w.shape[1]
    tc = LRU_CHUNK
    nc = seq // tc
    full = lambda shape: pl.BlockSpec(shape, lambda b, c: (0,) * len(shape))
    vec = lambda v: v.astype(F32).reshape(1, width)
    return pl.pallas_call(
        _lru_kernel,
        grid=(batch, nc),
        in_specs=[pl.BlockSpec((tc, 2 * width), lambda b, c: (b * nc + c, 0)),
                  full((4, width)), full((1, width)),
                  full(w_ga.shape), full((1, width)), full(w_gx.shape), full((1, width)),
                  full((1, width))],
        out_specs=pl.BlockSpec((tc, width), lambda b, c: (b * nc + c, 0)),
        out_shape=jax.ShapeDtypeStruct((t, width), BF16),
        scratch_shapes=[pltpu.VMEM((SUBLANES, width), F32), pltpu.VMEM((SUBLANES, width), F32),
                        pltpu.VMEM((tc, width), F32), pltpu.VMEM((tc, width), F32)],
        compiler_params=_cparams("parallel", "arbitrary"),
        name="lru_core",
    )(proj, conv_w, vec(conv_b), w_ga.astype(BF16), vec(b_ga), w_gx.astype(BF16), vec(b_gx), vec(lam))


def _attn_kernel(q_ref, k_ref, v_ref, o_ref, lse_ref, kp_scr, vp_scr):
    L, width = q_ref.shape
    n = pl.program_id(2)

    @pl.when(n == 0)
    def _():
        kp_scr[...] = jnp.zeros_like(kp_scr)
        vp_scr[...] = jnp.zeros_like(vp_scr)

    qi = lax.broadcasted_iota(jnp.int32, (L, 2 * L), 0)
    ki = lax.broadcasted_iota(jnp.int32, (L, 2 * L), 1)
    dist = qi + L - ki
    valid = (dist >= 0) & (dist <= L) & (ki + (n - 1) * L >= 0)
    lane_q = lax.broadcasted_iota(jnp.int32, (L, LANES), 1)
    lane_v = lax.broadcasted_iota(jnp.int32, (2 * L, LANES), 1)
    lse_tile = jnp.zeros((L, LANES), F32)
    zero = jnp.zeros((), BF16)
    heads_per_pair = LANES // ATTN_HEAD_DIM
    for j in range(width // LANES):
        sl = slice(j * LANES, (j + 1) * LANES)
        qp = q_ref[:, sl]
        kk = jnp.concatenate([kp_scr[:, sl], k_ref[:, sl]], axis=0)
        vv = jnp.concatenate([vp_scr[:, sl], v_ref[:, sl]], axis=0)
        acc = jnp.zeros((L, LANES), F32)
        for hh in range(heads_per_pair):
            in_head_q = lane_q < ATTN_HEAD_DIM if hh == 0 else lane_q >= ATTN_HEAD_DIM
            in_head_v = lane_v < ATTN_HEAD_DIM if hh == 0 else lane_v >= ATTN_HEAD_DIM
            s = _dot_nt(jnp.where(in_head_q, qp, zero), kk)
            s = jnp.where(valid, s, MASK_VALUE)
            m = jnp.max(s, axis=-1, keepdims=True)
            p = jnp.exp(s - m)
            l = jnp.sum(p, axis=-1, keepdims=True)
            o = _dot(p.astype(BF16), jnp.where(in_head_v, vv, zero))
            acc = acc + o / l
            lse_tile = jnp.where(lane_q == j * heads_per_pair + hh, m + jnp.log(l), lse_tile)
        o_ref[:, sl] = acc.astype(o_ref.dtype)
    lse_ref[...] = lse_tile
    kp_scr[...] = k_ref[...]
    vp_scr[...] = v_ref[...]


def _attn_group(qkv, batch, seq, g, dilation, span):
    t, qkv_dim = qkv.shape
    width = ATTN_HEADS * ATTN_HEAD_DIM
    m = seq // dilation
    nblk = m // span
    per_res = qkv_dim // width
    view = qkv.reshape(batch, m, dilation * qkv_dim)
    spec = lambda which: pl.BlockSpec((None, span, width),
                                      lambda b, r, n: (b, n, r * per_res + g * 3 + which))
    o, lse = pl.pallas_call(
        _attn_kernel,
        grid=(batch, dilation, nblk),
        in_specs=[spec(0), spec(1), spec(2)],
        out_specs=[pl.BlockSpec((None, span, width), lambda b, r, n: (b, n, r)),
                   pl.BlockSpec((None, span, LANES), lambda b, r, n: (b, n, r))],
        out_shape=[jax.ShapeDtypeStruct((batch, m, dilation * width), BF16),
                   jax.ShapeDtypeStruct((batch, m, dilation * LANES), F32)],
        scratch_shapes=[pltpu.VMEM((span, width), BF16), pltpu.VMEM((span, width), BF16)],
        compiler_params=_cparams("parallel", "parallel", "arbitrary"),
        name=f"attn_g{g}",
    )(view, view, view)
    return o.reshape(t, width), lse.reshape(t, LANES)


def _attn_out_kernel(o0_ref, o1_ref, o2_ref, l0_ref, l1_ref, l2_ref, w_ref, x_ref, gate_ref, gp_ref,
                     out_ref):
    tm, width = o0_ref.shape
    lane = lax.broadcasted_iota(jnp.int32, (tm, LANES), 1)
    first_head = lane < ATTN_HEAD_DIM
    lses = (l0_ref[...], l1_ref[...], l2_ref[...])
    outs = (o0_ref, o1_ref, o2_ref)
    parts = []
    for j in range(width // LANES):
        sl = slice(j * LANES, (j + 1) * LANES)
        ls = [jnp.where(first_head, l[:, 2 * j:2 * j + 1], l[:, 2 * j + 1:2 * j + 2]) for l in lses]
        mx = jnp.maximum(jnp.maximum(ls[0], ls[1]), ls[2])
        es = [jnp.exp(l - mx) for l in ls]
        num = sum(e * o[:, sl].astype(F32) for e, o in zip(es, outs))
        parts.append(num / (es[0] + es[1] + es[2]))
    o = jnp.concatenate(parts, axis=1).astype(BF16)
    out_ref[...] = _residual(x_ref[...], _dot(o, w_ref[...]), gate_ref[...], gp_ref[...])


def _attn_out(outs, lses, w, x, seq, modl, which, g_post):
    t, d = x.shape
    width = w.shape[0]
    tm = ROW_TILE
    per = seq // tm
    row = lambda n: pl.BlockSpec((tm, n), lambda i: (i, 0))
    return pl.pallas_call(
        _attn_out_kernel,
        grid=(t // tm,),
        in_specs=[row(width)] * 3 + [row(LANES)] * 3 + [
            pl.BlockSpec((width, d), lambda i: (0, 0)),
            row(d),
            pl.BlockSpec((None, 1, d), lambda i: ((i // per) * N_MOD + which, 0, 0)),
            pl.BlockSpec((1, d), lambda i: (0, 0))],
        out_specs=row(d),
        out_shape=jax.ShapeDtypeStruct((t, d), F32),
        compiler_params=_cparams("parallel"),
        name="attn_out",
    )(*outs, *lses, w, x, modl, g_post.reshape(1, d))


def _rope_tables(positions):
    half = ROPE_DIM // 2
    inv_freq = ROPE_THETA ** (-jnp.arange(0, ROPE_DIM, 2, dtype=F32) / ROPE_DIM)
    ang = positions.astype(F32).reshape(-1, 1) * inv_freq
    cos, sin = jnp.cos(ang), jnp.sin(ang)
    t = cos.shape[0]
    rest = ATTN_HEAD_DIM - ROPE_DIM
    zeros = jnp.zeros((t, half), F32)
    c = jnp.concatenate([cos, cos, jnp.ones((t, rest), F32)], axis=1)
    s1 = jnp.concatenate([-sin, zeros, jnp.zeros((t, rest), F32)], axis=1)
    s2 = jnp.concatenate([zeros, sin, jnp.zeros((t, rest), F32)], axis=1)
    reps = LANES // ATTN_HEAD_DIM
    return tuple(jnp.tile(v, (1, reps)) for v in (c, s1, s2))


def kernel(x, c, positions, ada_w, ada_b, norm_mix_pre, norm_mix_post, norm_mlp_pre, norm_mlp_post, mlp_w1, mlp_w2, ssd_w_in, ssd_conv_w, ssd_conv_b, ssd_dt_bias, ssd_a_log, ssd_d, ssd_norm, ssd_w_out, lru_w_in, lru_conv_w, lru_conv_b, lru_w_gate_a, lru_b_gate_a, lru_w_gate_x, lru_b_gate_x, lru_lambda, lru_w_out, attn_w_qkv, attn_w_out):
    batch, seq, d = x.shape
    depth = ada_w.shape[0]
    xf = x.reshape(batch * seq, d)
    mod = _modulation(c, ada_w, ada_b)
    rope = _rope_tables(positions)
    for layer in range(depth):
        modl = mod[layer].reshape(batch * N_MOD, 1, d)
        kind, occ = layer % N_MIXERS, layer // N_MIXERS
        if kind == 0:
            w_in = ssd_w_in[occ]
            n_heads = ssd_dt_bias.shape[1]
            n_main = w_in.shape[1] - n_heads
            w_dt = jnp.pad(w_in[:, n_main:], ((0, 0), (0, LANES - n_heads))).astype(BF16)
            zx, dt_raw = _inproj(xf, seq, norm_mix_pre[layer], modl, 0, w_in[:, :n_main].astype(BF16),
                                 w_dt=w_dt)
            y = _ssd_core(zx, dt_raw, batch, seq, ssd_conv_w[occ], ssd_conv_b[occ], ssd_dt_bias[occ],
                          ssd_a_log[occ], ssd_d[occ], ssd_norm[occ])
            xf = _outproj(y, ssd_w_out[occ].astype(BF16), xf, seq, modl, 2, norm_mix_post[layer])
        elif kind == 1:
            proj = _inproj(xf, seq, norm_mix_pre[layer], modl, 0, lru_w_in[occ].astype(BF16))
            y = _lru_core(proj, batch, seq, lru_conv_w[occ], lru_conv_b[occ], lru_w_gate_a[occ],
                          lru_b_gate_a[occ], lru_w_gate_x[occ], lru_b_gate_x[occ], lru_lambda[occ])
            xf = _outproj(y, lru_w_out[occ].astype(BF16), xf, seq, modl, 2, norm_mix_post[layer])
        else:
            qkv = _inproj(xf, seq, norm_mix_pre[layer], modl, 0, attn_w_qkv[occ].astype(BF16),
                          rope=rope, q_scale=ATTN_HEAD_DIM ** -0.5)
            outs, lses = [], []
            for g, (window, dilation) in enumerate(ATTN_CONFIGS):
                o, lse = _attn_group(qkv, batch, seq, g, dilation, window // dilation)
                outs.append(o)
                lses.append(lse)
            xf = _attn_out(outs, lses, attn_w_out[occ].astype(BF16), xf, seq, modl, 2,
                           norm_mix_post[layer])
        xf = _mlp(xf, seq, modl, norm_mlp_pre[layer], norm_mlp_post[layer],
                  mlp_w1[layer].astype(BF16), mlp_w2[layer].astype(BF16))
    return xf.reshape(batch, seq, d)
```

```python
import functools

import jax
import jax.numpy as jnp
from jax import lax
from jax.experimental import pallas as pl
from jax.experimental.pallas import tpu as pltpu

F32 = jnp.float32
BF16 = jnp.bfloat16

NORM_EPS = 1e-6
N_MIXERS = 3
N_MOD = 6

SSD_HEAD_DIM = 64
SSD_N_GROUPS = 8
SSD_D_STATE = 128
SSD_CHUNK = 128

LRU_N_BLOCKS = 4
LRU_C = 8.0
LRU_CHUNK = 256

ATTN_HEAD_DIM = 64
ATTN_HEADS = 8
ATTN_CONFIGS = ((128, 1), (512, 4), (2048, 16))
ROPE_THETA = 500000.0
ROPE_DIM = ATTN_HEAD_DIM // 4
MASK_VALUE = -1e30

LANES = 128
SUBLANES = 8
VMEM_LIMIT = 56 * 1024 * 1024

ROW_TILE = 512
COL_TILE = 512


def _cparams(*sem):
    return pltpu.CompilerParams(dimension_semantics=sem, vmem_limit_bytes=VMEM_LIMIT)


def _dot(a, b):
    return jnp.dot(a, b, preferred_element_type=F32)


def _dot_nt(a, b):
    return lax.dot_general(a, b, (((1,), (1,)), ((), ())), preferred_element_type=F32)


def _split3(v):
    hi = v.astype(BF16)
    r1 = v - hi.astype(F32)
    mid = r1.astype(BF16)
    lo = (r1 - mid.astype(F32)).astype(BF16)
    return hi, mid, lo


def _rms(y):
    return y * lax.rsqrt(jnp.mean(y * y, axis=-1, keepdims=True) + NORM_EPS)


def _norm_mod(x, g, sc, sh):
    return _rms(x) * g * (1.0 + sc) + sh


def _residual(x, y, gate, g_post):
    return x + (1.0 + gate) * (_rms(y) * g_post)


def _softplus(x):
    return jnp.maximum(x, 0.0) + jnp.log1p(jnp.exp(-jnp.abs(x)))


def _silu(x):
    return x * jax.nn.sigmoid(x)


def _causal_conv(x, tail, w, b):
    row8 = lax.broadcasted_iota(jnp.int32, (SUBLANES, x.shape[1]), 0)
    acc = x * w[3:4, :] + b
    for j in (1, 2, 3):
        sh = pltpu.roll(x, j, 0)
        top = jnp.where(row8 < j, pltpu.roll(tail, j, 0), sh[0:SUBLANES])
        sh = jnp.concatenate([top, sh[SUBLANES:]], axis=0)
        acc = acc + sh * w[3 - j:4 - j, :]
    return acc


def _resident(shape):
    return pl.BlockSpec(shape, lambda *_: (0,) * len(shape), pipeline_mode=pl.Buffered(1))


def _mod_kernel(c_ref, w_ref, b_ref, o_ref):
    c = c_ref[...]
    a_hi, a_mid, _ = _split3(_silu(c))
    w = w_ref[...]
    w_hi = w.astype(BF16)
    w_lo = (w - w_hi.astype(F32)).astype(BF16)
    o_ref[...] = _dot(a_hi, w_hi) + _dot(a_hi, w_lo) + _dot(a_mid, w_hi) + b_ref[...]


def _modulation(c, ada_w, ada_b):
    depth, d, e = ada_w.shape
    b = c.shape[0]
    tn = 1024
    return pl.pallas_call(
        _mod_kernel,
        grid=(depth, e // tn),
        in_specs=[pl.BlockSpec((b, d), lambda l, j: (0, 0)),
                  pl.BlockSpec((None, d, tn), lambda l, j: (l, 0, j)),
                  pl.BlockSpec((None, 1, tn), lambda l, j: (l, 0, j))],
        out_specs=pl.BlockSpec((None, b, tn), lambda l, j: (l, 0, j)),
        out_shape=jax.ShapeDtypeStruct((depth, b, e), F32),
        compiler_params=_cparams("parallel", "parallel"),
        name="adaln_mod",
    )(c, ada_w, ada_b.reshape(depth, 1, e))


def _rope(y, rc, rs1, rs2):
    reps = y.shape[1] // LANES
    tile = lambda t: jnp.concatenate([t] * reps, axis=1)
    n = y.shape[1]
    half = ROPE_DIM // 2
    return y * tile(rc) + pltpu.roll(y, n - half, 1) * tile(rs1) + pltpu.roll(y, half, 1) * tile(rs2)


def _inproj_kernel(*refs, n_cols, has_dt, dilations, q_scale):
    x_ref, g_ref, sc_ref, sh_ref, w_ref = refs[:5]
    pos = 5
    if has_dt:
        wdt_ref = refs[pos]
        pos += 1
    if dilations:
        rc_ref, rs1_ref, rs2_ref = refs[pos:pos + 3]
        pos += 3
    n_outs = len(dilations) if dilations else 1
    out_refs = refs[pos:pos + n_outs]
    pos += n_outs
    if has_dt:
        odt_ref = refs[pos]
        pos += 1
    h_scr = refs[pos]
    tm = x_ref.shape[0]
    tn = COL_TILE

    h_scr[...] = _norm_mod(x_ref[...], g_ref[...], sc_ref[...], sh_ref[...]).astype(BF16)
    if has_dt:
        odt_ref[...] = _dot(h_scr[...], wdt_ref[...])
    for c0 in range(0, n_cols, tn):
        acc = _dot(h_scr[...], w_ref[:, c0:c0 + tn])
        if not dilations:
            out_refs[0][:, c0:c0 + tn] = acc.astype(BF16)
            continue
        grp, kind = divmod(c0 // tn, 3)
        if kind < 2:
            acc = _rope(acc, rc_ref[...], rs1_ref[...], rs2_ref[...])
            if kind == 0:
                acc = acc * q_scale
        o_ref, d = out_refs[grp], dilations[grp]
        if d == 1:
            o_ref[0, :, kind * tn:(kind + 1) * tn] = acc.astype(BF16)
        else:
            tile_scr = refs[pos + 1]
            for j in range(tn // LANES):
                tile_scr[j] = acc[:, j * LANES:(j + 1) * LANES]
            for r in range(d):
                for j in range(tn // LANES):
                    o_ref[r, :, kind * tn + j * LANES:kind * tn + (j + 1) * LANES] = (
                        tile_scr[j, pl.ds(r, tm // d, stride=d), :].astype(BF16))


def _inproj(x, seq, g, modl, which, w, w_dt=None, rope=None, dilations=None, q_scale=1.0):
    t, d = x.shape
    n = w.shape[1]
    tm = ROW_TILE
    per = seq // tm
    batch = t // seq
    in_specs = [pl.BlockSpec((tm, d), lambda i: (i, 0)),
                pl.BlockSpec((1, d), lambda i: (0, 0)),
                pl.BlockSpec((None, 1, d), lambda i: ((i // per) * N_MOD + which + 1, 0, 0)),
                pl.BlockSpec((None, 1, d), lambda i: ((i // per) * N_MOD + which, 0, 0)),
                _resident((d, n))]
    args = [x, g.reshape(1, d), modl, modl, w]
    scratch = [pltpu.VMEM((tm, d), BF16)]
    if w_dt is not None:
        in_specs.append(_resident((d, LANES)))
        args.append(w_dt)
    if dilations:
        for tab in rope:
            in_specs.append(pl.BlockSpec((tm, LANES), lambda i: (i, 0)))
            args.append(tab)
        gw = n // len(dilations)
        out_specs = [pl.BlockSpec((None, dl, None, tm // dl, gw), lambda i: (i // per, 0, i % per, 0, 0))
                     for dl in dilations]
        out_shape = [jax.ShapeDtypeStruct((batch, dl, per, tm // dl, gw), BF16) for dl in dilations]
        scratch.append(pltpu.VMEM((COL_TILE // LANES, tm, LANES), F32))
    else:
        out_specs = [pl.BlockSpec((tm, n), lambda i: (i, 0))]
        out_shape = [jax.ShapeDtypeStruct((t, n), BF16)]
    if w_dt is not None:
        out_specs.append(pl.BlockSpec((tm, LANES), lambda i: (i, 0)))
        out_shape.append(jax.ShapeDtypeStruct((t, LANES), F32))
    res = pl.pallas_call(
        functools.partial(_inproj_kernel, n_cols=n, has_dt=w_dt is not None,
                          dilations=tuple(dilations) if dilations else None, q_scale=q_scale),
        grid=(t // tm,),
        in_specs=in_specs,
        out_specs=out_specs,
        out_shape=out_shape,
        scratch_shapes=scratch,
        compiler_params=_cparams("parallel"),
        name="inproj",
    )(*args)
    return res if len(res) > 1 else res[0]


def _outproj_kernel(y_ref, w_ref, x_ref, gate_ref, gp_ref, o_ref):
    o_ref[...] = _residual(x_ref[...], _dot(y_ref[...], w_ref[...]), gate_ref[...], gp_ref[...])


def _outproj(y, w, x, seq, modl, which, g_post):
    t, kdim = y.shape
    d = w.shape[1]
    tm = ROW_TILE
    per = seq // tm
    return pl.pallas_call(
        _outproj_kernel,
        grid=(t // tm,),
        in_specs=[pl.BlockSpec((tm, kdim), lambda i: (i, 0)),
                  _resident((kdim, d)),
                  pl.BlockSpec((tm, d), lambda i: (i, 0)),
                  pl.BlockSpec((None, 1, d), lambda i: ((i // per) * N_MOD + which, 0, 0)),
                  pl.BlockSpec((1, d), lambda i: (0, 0))],
        out_specs=pl.BlockSpec((tm, d), lambda i: (i, 0)),
        out_shape=jax.ShapeDtypeStruct((t, d), F32),
        compiler_params=_cparams("parallel"),
        name="outproj",
    )(y, w, x, modl, g_post.reshape(1, d))


def _mlp_kernel(x_ref, g_ref, sc_ref, sh_ref, w1_ref, w2_ref, gate_ref, gp_ref, o_ref, h_scr):
    h_scr[...] = _norm_mod(x_ref[...], g_ref[...], sc_ref[...], sh_ref[...]).astype(BF16)
    tf = COL_TILE
    acc = None
    for f0 in range(0, w1_ref.shape[1], tf):
        a = jnp.maximum(_dot(h_scr[...], w1_ref[:, f0:f0 + tf]), 0.0)
        part = _dot((a * a).astype(BF16), w2_ref[f0:f0 + tf, :])
        acc = part if acc is None else acc + part
    o_ref[...] = _residual(x_ref[...], acc, gate_ref[...], gp_ref[...])


def _mlp(x, seq, modl, g_pre, g_post, w1, w2):
    t, d = x.shape
    dff = w1.shape[1]
    tm = ROW_TILE
    per = seq // tm
    mod_spec = lambda which: pl.BlockSpec((None, 1, d), lambda i: ((i // per) * N_MOD + which, 0, 0))
    return pl.pallas_call(
        _mlp_kernel,
        grid=(t // tm,),
        in_specs=[pl.BlockSpec((tm, d), lambda i: (i, 0)),
                  pl.BlockSpec((1, d), lambda i: (0, 0)),
                  mod_spec(4), mod_spec(3),
                  _resident((d, dff)), _resident((dff, d)),
                  mod_spec(5),
                  pl.BlockSpec((1, d), lambda i: (0, 0))],
        out_specs=pl.BlockSpec((tm, d), lambda i: (i, 0)),
        out_shape=jax.ShapeDtypeStruct((t, d), F32),
        scratch_shapes=[pltpu.VMEM((tm, d), BF16)],
        compiler_params=_cparams("parallel"),
        name="mlp",
    )(x, g_pre.reshape(1, d), modl, modl, w1, w2, modl, g_post.reshape(1, d))


def _ssd_kernel(zx_ref, dt_ref, cw_ref, cb_ref, dtb_ref, alog_ref, dsk_ref, ng_ref, exp_ref,
                y_ref, state_scr, tail_scr, act_scr, *, d_inner):
    L, N, G = SSD_CHUNK, SSD_D_STATE, SSD_N_GROUPS
    gw = d_inner // G
    conv_dim = d_inner + 2 * G * N
    c = pl.program_id(1)

    @pl.when(c == 0)
    def _():
        state_scr[...] = jnp.zeros_like(state_scr)
        tail_scr[...] = jnp.zeros_like(tail_scr)

    strip = 512
    for s0 in range(0, conv_dim, strip):
        cols = slice(s0, s0 + strip)
        xin = zx_ref[:, d_inner + s0:d_inner + s0 + strip].astype(F32)
        y = _causal_conv(xin, tail_scr[:, cols], cw_ref[:, cols], cb_ref[:, cols])
        tail_scr[:, cols] = xin[L - SUBLANES:L]
        act_scr[:, cols] = _silu(y)

    dt = _softplus(dt_ref[...] + dtb_ref[...])
    a = -jnp.exp(alog_ref[...])
    row = lax.broadcasted_iota(jnp.int32, (L, L), 0)
    col = lax.broadcasted_iota(jnp.int32, (L, L), 1)
    tri = row >= col
    tril = jnp.where(tri, 1.0, 0.0).astype(BF16)
    a_cs = sum(_dot(tril, term) for term in _split3(dt * a))
    a_cs_t = a_cs.T
    dt_t = dt.T
    e_cs = jnp.exp(a_cs)
    dte = jnp.exp(a_cs[L - 1:L, :] - a_cs) * dt
    expand = exp_ref[...]
    hi, mid, _ = _split3(e_cs)
    e_cs_x = _dot(hi, expand) + _dot(mid, expand)
    hi, mid, _ = _split3(dte)
    dte_x = _dot(hi, expand) + _dot(mid, expand)

    lane = lax.broadcasted_iota(jnp.int32, (L, LANES), 1)
    first_head = lane < SSD_HEAD_DIM
    heads_per_pair = LANES // SSD_HEAD_DIM
    for g in range(G):
        gs = slice(g * gw, (g + 1) * gw)
        xs_g = act_scr[:, gs]
        bm_g = act_scr[:, d_inner + g * N:d_inner + (g + 1) * N]
        cm_g = act_scr[:, d_inner + G * N + g * N:d_inner + G * N + (g + 1) * N].astype(BF16)
        cb = _dot_nt(cm_g, bm_g.astype(BF16))
        st_prev = state_scr[g]
        y_off = _dot(cm_g, st_prev.astype(BF16)) * e_cs_x[:, gs]
        w_g = (xs_g * dte_x[:, gs]).astype(BF16)
        new_state = _dot(bm_g.T.astype(BF16), w_g)
        state_scr[g] = st_prev * e_cs_x[L - 1:L, gs] + new_state
        y_parts = []
        for q in range(gw // LANES):
            xs_p = xs_g[:, q * LANES:(q + 1) * LANES]
            ms = []
            for hh in range(heads_per_pair):
                h = (g * gw + q * LANES) // SSD_HEAD_DIM + hh
                seg = a_cs[:, h:h + 1] - a_cs_t[h:h + 1, :]
                decay = jnp.exp(jnp.where(tri, seg, -jnp.inf))
                ms.append((cb * decay * dt_t[h:h + 1, :]).astype(BF16))
            m_cat = jnp.concatenate(ms, axis=1)
            x_cat = jnp.concatenate([jnp.where(first_head, xs_p, 0.0),
                                     jnp.where(first_head, 0.0, xs_p)], axis=0).astype(BF16)
            y_parts.append(_dot(m_cat, x_cat))
        y = jnp.concatenate(y_parts, axis=1) + y_off + xs_g * dsk_ref[:, gs]
        y = y * _silu(zx_ref[:, gs].astype(F32))
        y_ref[:, gs] = (_rms(y) * ng_ref[:, gs]).astype(y_ref.dtype)


def _ssd_core(zx, dt_raw, batch, seq, conv_w, conv_b, dt_bias, a_log, d_skip, norm_g):
    t = zx.shape[0]
    n_heads = dt_bias.shape[0]
    d_inner = n_heads * SSD_HEAD_DIM
    conv_dim = conv_w.shape[1]
    L = SSD_CHUNK
    nc = seq // L
    pad = lambda v: jnp.pad(v.astype(F32), (0, LANES - n_heads)).reshape(1, LANES)
    expand = (jnp.arange(LANES)[:, None] == (jnp.arange(d_inner)[None, :] // SSD_HEAD_DIM)).astype(BF16)
    dsk_x = jnp.repeat(d_skip.astype(F32), SSD_HEAD_DIM).reshape(1, d_inner)
    return pl.pallas_call(
        functools.partial(_ssd_kernel, d_inner=d_inner),
        grid=(batch, nc),
        in_specs=[pl.BlockSpec((L, zx.shape[1]), lambda b, c: (b * nc + c, 0)),
                  pl.BlockSpec((L, LANES), lambda b, c: (b * nc + c, 0)),
                  _resident((4, conv_dim)), _resident((1, conv_dim)), _resident((1, LANES)),
                  _resident((1, LANES)), _resident((1, d_inner)), _resident((1, d_inner)),
                  _resident((LANES, d_inner))],
        out_specs=pl.BlockSpec((L, d_inner), lambda b, c: (b * nc + c, 0)),
        out_shape=jax.ShapeDtypeStruct((t, d_inner), BF16),
        scratch_shapes=[pltpu.VMEM((SSD_N_GROUPS, SSD_D_STATE, d_inner // SSD_N_GROUPS), F32),
                        pltpu.VMEM((SUBLANES, conv_dim), F32),
                        pltpu.VMEM((L, conv_dim), F32)],
        compiler_params=_cparams("parallel", "arbitrary"),
        name="ssd_core",
    )(zx, dt_raw, conv_w, conv_b.reshape(1, conv_dim), pad(dt_bias), pad(a_log), dsk_x,
      norm_g.reshape(1, d_inner), expand)


def _lru_kernel(p_ref, cw_ref, cb_ref, wga_ref, bga_ref, wgx_ref, bgx_ref, lam_ref, y_ref,
                tail_scr, h_scr, a_scr, u_scr):
    tc, width = y_ref.shape
    blk = width // LRU_N_BLOCKS
    c = pl.program_id(1)

    @pl.when(c == 0)
    def _():
        tail_scr[...] = jnp.zeros_like(tail_scr)
        h_scr[...] = jnp.zeros_like(h_scr)

    xr = p_ref[:, width:2 * width].astype(F32)
    xc = _causal_conv(xr, tail_scr[...], cw_ref[...], cb_ref[...])
    tail_scr[...] = xr[tc - SUBLANES:tc]
    neg_sp = _softplus(-lam_ref[...])
    for k in range(LRU_N_BLOCKS):
        sl = slice(k * blk, (k + 1) * blk)
        xb = xc[:, sl]
        xb16 = xb.astype(BF16)
        r = jax.nn.sigmoid(_dot(xb16, wga_ref[k]) + bga_ref[:, sl])
        i = jax.nn.sigmoid(_dot(xb16, wgx_ref[k]) + bgx_ref[:, sl])
        log_a = -LRU_C * r * neg_sp[:, sl]
        a_scr[:, sl] = jnp.exp(log_a)
        u_scr[:, sl] = jnp.sqrt(1.0 - jnp.exp(2.0 * log_a)) * (i * xb)

    row = lax.broadcasted_iota(jnp.int32, (SUBLANES, width), 0)

    def body(t, h):
        rows = pl.ds(pl.multiple_of(t * SUBLANES, SUBLANES), SUBLANES)
        a = a_scr[rows, :]
        u = u_scr[rows, :]
        for k in (1, 2, 4):
            keep = row >= k
            a_sh = jnp.where(keep, pltpu.roll(a, k, 0), 1.0)
            u_sh = jnp.where(keep, pltpu.roll(u, k, 0), 0.0)
            u = a * u_sh + u
            a = a * a_sh
        hh = a * h + u
        u_scr[rows, :] = hh
        return jnp.broadcast_to(hh[SUBLANES - 1:SUBLANES, :], hh.shape)

    h_scr[...] = lax.fori_loop(0, tc // SUBLANES, body, h_scr[...])
    gate = p_ref[:, 0:width].astype(F32)
    y_ref[...] = (u_scr[...] * jax.nn.gelu(gate, approximate=True)).astype(y_ref.dtype)


def _lru_core(proj, batch, seq, conv_w, conv_b, w_ga, b_ga, w_gx, b_gx, lam):
    t = proj.shape[0]
    width = conv_w.shape[1]
    tc = LRU_CHUNK
    nc = seq // tc
    vec = lambda v: v.astype(F32).reshape(1, width)
    return pl.pallas_call(
        _lru_kernel,
        grid=(batch, nc),
        in_specs=[pl.BlockSpec((tc, 2 * width), lambda b, c: (b * nc + c, 0)),
                  _resident((4, width)), _resident((1, width)),
                  _resident(w_ga.shape), _resident((1, width)), _resident(w_gx.shape),
                  _resident((1, width)), _resident((1, width))],
        out_specs=pl.BlockSpec((tc, width), lambda b, c: (b * nc + c, 0)),
        out_shape=jax.ShapeDtypeStruct((t, width), BF16),
        scratch_shapes=[pltpu.VMEM((SUBLANES, width), F32), pltpu.VMEM((SUBLANES, width), F32),
                        pltpu.VMEM((tc, width), F32), pltpu.VMEM((tc, width), F32)],
        compiler_params=_cparams("parallel", "arbitrary"),
        name="lru_core",
    )(proj, conv_w, vec(conv_b), w_ga.astype(BF16), vec(b_ga), w_gx.astype(BF16), vec(b_gx), vec(lam))


def _attn_kernel(q_ref, k_ref, v_ref, o_ref, lse_ref, kp_scr, vp_scr, *tile_scr, dilation):
    L, width = q_ref.shape[1:]
    n = pl.program_id(1)

    @pl.when(n == 0)
    def _():
        kp_scr[...] = jnp.zeros_like(kp_scr)
        vp_scr[...] = jnp.zeros_like(vp_scr)

    qi = lax.broadcasted_iota(jnp.int32, (L, 2 * L), 0)
    ki = lax.broadcasted_iota(jnp.int32, (L, 2 * L), 1)
    dist = qi + L - ki
    valid = (dist >= 0) & (dist <= L) & (ki + (n - 1) * L >= 0)
    lane_q = lax.broadcasted_iota(jnp.int32, (L, LANES), 1)
    lane_v = lax.broadcasted_iota(jnp.int32, (2 * L, LANES), 1)
    zero = jnp.zeros((), BF16)
    heads_per_pair = LANES // ATTN_HEAD_DIM

    def residue(r, carry):
        lse_tile = jnp.zeros((L, LANES), F32)
        o_parts = []
        for j in range(width // LANES):
            sl = slice(j * LANES, (j + 1) * LANES)
            qp = q_ref[r, :, sl]
            kk = jnp.concatenate([kp_scr[r, :, sl], k_ref[r, :, sl]], axis=0)
            vv = jnp.concatenate([vp_scr[r, :, sl], v_ref[r, :, sl]], axis=0)
            acc = jnp.zeros((L, LANES), F32)
            for hh in range(heads_per_pair):
                in_head_q = lane_q < ATTN_HEAD_DIM if hh == 0 else lane_q >= ATTN_HEAD_DIM
                in_head_v = lane_v < ATTN_HEAD_DIM if hh == 0 else lane_v >= ATTN_HEAD_DIM
                s = _dot_nt(jnp.where(in_head_q, qp, zero), kk)
                s = jnp.where(valid, s, MASK_VALUE)
                m = jnp.max(s, axis=-1, keepdims=True)
                p = jnp.exp(s - m)
                l = jnp.sum(p, axis=-1, keepdims=True)
                o = _dot(p.astype(BF16), jnp.where(in_head_v, vv, zero))
                acc = acc + o / l
                lse_tile = jnp.where(lane_q == j * heads_per_pair + hh, m + jnp.log(l), lse_tile)
            o_parts.append(acc)
        if dilation == 1:
            o_ref[...] = jnp.concatenate(o_parts, axis=1).astype(o_ref.dtype)
            lse_ref[...] = lse_tile
        else:
            o_scr, lse_scr = tile_scr
            for j, part in enumerate(o_parts):
                o_scr[j, pl.ds(r, L, stride=dilation), :] = part
            lse_scr[pl.ds(r, L, stride=dilation), :] = lse_tile
        kp_scr[r] = k_ref[r]
        vp_scr[r] = v_ref[r]
        return carry

    if dilation == 1:
        residue(0, 0)
    else:
        lax.fori_loop(0, dilation, residue, 0)
        o_scr, lse_scr = tile_scr
        for j in range(width // LANES):
            o_ref[:, j * LANES:(j + 1) * LANES] = o_scr[j].astype(o_ref.dtype)
        lse_ref[...] = lse_scr[...]


def _attn_group(qkv_g, batch, seq, dilation, span):
    width = ATTN_HEADS * ATTN_HEAD_DIM
    m = seq // dilation
    nblk = m // span
    t = batch * seq
    rows = span * dilation
    spec = lambda which: pl.BlockSpec((None, dilation, span, width), lambda b, n: (b, 0, n, which))
    scratch = [pltpu.VMEM((dilation, span, width), BF16), pltpu.VMEM((dilation, span, width), BF16)]
    if dilation > 1:
        scratch += [pltpu.VMEM((width // LANES, rows, LANES), F32), pltpu.VMEM((rows, LANES), F32)]
    return pl.pallas_call(
        functools.partial(_attn_kernel, dilation=dilation),
        grid=(batch, nblk),
        in_specs=[spec(0), spec(1), spec(2)],
        out_specs=[pl.BlockSpec((rows, width), lambda b, n: (b * nblk + n, 0)),
                   pl.BlockSpec((rows, LANES), lambda b, n: (b * nblk + n, 0))],
        out_shape=[jax.ShapeDtypeStruct((t, width), BF16),
                   jax.ShapeDtypeStruct((t, LANES), F32)],
        scratch_shapes=scratch,
        compiler_params=_cparams("parallel", "arbitrary"),
        name=f"attn_d{dilation}",
    )(qkv_g, qkv_g, qkv_g)


def _attn_out_kernel(o0_ref, o1_ref, o2_ref, l0_ref, l1_ref, l2_ref, w_ref, x_ref, gate_ref, gp_ref,
                     out_ref):
    tm, width = o0_ref.shape
    lane = lax.broadcasted_iota(jnp.int32, (tm, LANES), 1)
    first_head = lane < ATTN_HEAD_DIM
    lses = (l0_ref[...], l1_ref[...], l2_ref[...])
    outs = (o0_ref, o1_ref, o2_ref)
    parts = []
    for j in range(width // LANES):
        sl = slice(j * LANES, (j + 1) * LANES)
        ls = [jnp.where(first_head, l[:, 2 * j:2 * j + 1], l[:, 2 * j + 1:2 * j + 2]) for l in lses]
        mx = jnp.maximum(jnp.maximum(ls[0], ls[1]), ls[2])
        es = [jnp.exp(l - mx) for l in ls]
        num = sum(e * o[:, sl].astype(F32) for e, o in zip(es, outs))
        parts.append(num / (es[0] + es[1] + es[2]))
    o = jnp.concatenate(parts, axis=1).astype(BF16)
    out_ref[...] = _residual(x_ref[...], _dot(o, w_ref[...]), gate_ref[...], gp_ref[...])


def _attn_out(outs, lses, w, x, seq, modl, which, g_post):
    t, d = x.shape
    width = w.shape[0]
    tm = ROW_TILE
    per = seq // tm
    row = lambda n: pl.BlockSpec((tm, n), lambda i: (i, 0))
    return pl.pallas_call(
        _attn_out_kernel,
        grid=(t // tm,),
        in_specs=[row(width)] * 3 + [row(LANES)] * 3 + [
            _resident((width, d)),
            row(d),
            pl.BlockSpec((None, 1, d), lambda i: ((i // per) * N_MOD + which, 0, 0)),
            pl.BlockSpec((1, d), lambda i: (0, 0))],
        out_specs=row(d),
        out_shape=jax.ShapeDtypeStruct((t, d), F32),
        compiler_params=_cparams("parallel"),
        name="attn_out",
    )(*outs, *lses, w, x, modl, g_post.reshape(1, d))


def _rope_tables(positions):
    half = ROPE_DIM // 2
    inv_freq = ROPE_THETA ** (-jnp.arange(0, ROPE_DIM, 2, dtype=F32) / ROPE_DIM)
    ang = positions.astype(F32).reshape(-1, 1) * inv_freq
    cos, sin = jnp.cos(ang), jnp.sin(ang)
    t = cos.shape[0]
    rest = ATTN_HEAD_DIM - ROPE_DIM
    zeros = jnp.zeros((t, half), F32)
    c = jnp.concatenate([cos, cos, jnp.ones((t, rest), F32)], axis=1)
    s1 = jnp.concatenate([-sin, zeros, jnp.zeros((t, rest), F32)], axis=1)
    s2 = jnp.concatenate([zeros, sin, jnp.zeros((t, rest), F32)], axis=1)
    reps = LANES // ATTN_HEAD_DIM
    return tuple(jnp.tile(v, (1, reps)) for v in (c, s1, s2))


def _attention_layer(xf, batch, seq, g_pre, g_post, modl, rope, w_qkv, w_out):
    dilations = [d for _, d in ATTN_CONFIGS]
    qkvs = _inproj(xf, seq, g_pre, modl, 0, w_qkv, rope=rope, dilations=dilations,
                   q_scale=ATTN_HEAD_DIM ** -0.5)
    outs, lses = [], []
    for qkv_g, (window, dilation) in zip(qkvs, ATTN_CONFIGS):
        qkv_g = qkv_g.reshape(batch, dilation, seq // dilation, qkv_g.shape[-1])
        o, lse = _attn_group(qkv_g, batch, seq, dilation, window // dilation)
        outs.append(o)
        lses.append(lse)
    return _attn_out(outs, lses, w_out, xf, seq, modl, 2, g_post)


def kernel(x, c, positions, ada_w, ada_b, norm_mix_pre, norm_mix_post, norm_mlp_pre, norm_mlp_post, mlp_w1, mlp_w2, ssd_w_in, ssd_conv_w, ssd_conv_b, ssd_dt_bias, ssd_a_log, ssd_d, ssd_norm, ssd_w_out, lru_w_in, lru_conv_w, lru_conv_b, lru_w_gate_a, lru_b_gate_a, lru_w_gate_x, lru_b_gate_x, lru_lambda, lru_w_out, attn_w_qkv, attn_w_out):
    batch, seq, d = x.shape
    depth = ada_w.shape[0]
    xf = x.reshape(batch * seq, d)
    mod = _modulation(c, ada_w, ada_b)
    rope = _rope_tables(positions)
    for layer in range(depth):
        modl = mod[layer].reshape(batch * N_MOD, 1, d)
        kind, occ = layer % N_MIXERS, layer // N_MIXERS
        if kind == 0:
            w_in = ssd_w_in[occ]
            n_heads = ssd_dt_bias.shape[1]
            n_main = w_in.shape[1] - n_heads
            w_dt = jnp.pad(w_in[:, n_main:], ((0, 0), (0, LANES - n_heads))).astype(BF16)
            zx, dt_raw = _inproj(xf, seq, norm_mix_pre[layer], modl, 0, w_in[:, :n_main].astype(BF16),
                                 w_dt=w_dt)
            y = _ssd_core(zx, dt_raw, batch, seq, ssd_conv_w[occ], ssd_conv_b[occ], ssd_dt_bias[occ],
                          ssd_a_log[occ], ssd_d[occ], ssd_norm[occ])
            xf = _outproj(y, ssd_w_out[occ].astype(BF16), xf, seq, modl, 2, norm_mix_post[layer])
        elif kind == 1:
            proj = _inproj(xf, seq, norm_mix_pre[layer], modl, 0, lru_w_in[occ].astype(BF16))
            y = _lru_core(proj, batch, seq, lru_conv_w[occ], lru_conv_b[occ], lru_w_gate_a[occ],
                          lru_b_gate_a[occ], lru_w_gate_x[occ], lru_b_gate_x[occ], lru_lambda[occ])
            xf = _outproj(y, lru_w_out[occ].astype(BF16), xf, seq, modl, 2, norm_mix_post[layer])
        else:
            xf = _attention_layer(xf, batch, seq, norm_mix_pre[layer], norm_mix_post[layer], modl, rope,
                                  attn_w_qkv[occ].astype(BF16), attn_w_out[occ].astype(BF16))
        xf = _mlp(xf, seq, modl, norm_mlp_pre[layer], norm_mlp_post[layer],
                  mlp_w1[layer].astype(BF16), mlp_w2[layer].astype(BF16))
    return xf.reshape(batch, seq, d)
```

```python
import functools

import jax
import jax.numpy as jnp
from jax import lax
from jax.experimental import pallas as pl
from jax.experimental.pallas import tpu as pltpu

F32 = jnp.float32
BF16 = jnp.bfloat16

NORM_EPS = 1e-6
N_MIXERS = 3
N_MOD = 6

SSD_HEAD_DIM = 64
SSD_N_GROUPS = 8
SSD_D_STATE = 128
SSD_CHUNK = 128

LRU_N_BLOCKS = 4
LRU_C = 8.0
LRU_CHUNK = 256

ATTN_HEAD_DIM = 64
ATTN_HEADS = 8
ATTN_CONFIGS = ((128, 1), (512, 4), (2048, 16))
ROPE_THETA = 500000.0
ROPE_DIM = ATTN_HEAD_DIM // 4
MASK_VALUE = -1e30

LANES = 128
SUBLANES = 8
VMEM_LIMIT = 56 * 1024 * 1024

ROW_TILE = 512
COL_TILE = 512


def _cparams(*sem):
    return pltpu.CompilerParams(dimension_semantics=sem, vmem_limit_bytes=VMEM_LIMIT)


def _dot(a, b):
    return jnp.dot(a, b, preferred_element_type=F32)


def _dot_nt(a, b):
    return lax.dot_general(a, b, (((1,), (1,)), ((), ())), preferred_element_type=F32)


def _split3(v):
    hi = v.astype(BF16)
    r1 = v - hi.astype(F32)
    mid = r1.astype(BF16)
    lo = (r1 - mid.astype(F32)).astype(BF16)
    return hi, mid, lo


def _rms(y):
    return y * lax.rsqrt(jnp.mean(y * y, axis=-1, keepdims=True) + NORM_EPS)


def _norm_mod(x, g, sc, sh):
    return _rms(x) * g * (1.0 + sc) + sh


def _residual(x, y, gate, g_post):
    return x + (1.0 + gate) * (_rms(y) * g_post)


def _softplus(x):
    return jnp.maximum(x, 0.0) + jnp.log1p(jnp.exp(-jnp.abs(x)))


def _sigmoid(x):
    return 0.5 + 0.5 * jnp.tanh(0.5 * x)


def _silu(x):
    h = 0.5 * x
    return h + h * jnp.tanh(h)


CONV_TAPS = 4


def _causal_conv(x, halo, w, b):
    row8 = lax.broadcasted_iota(jnp.int32, (SUBLANES, x.shape[1]), 0)
    acc = x * w[CONV_TAPS - 1:CONV_TAPS, :] + b
    for j in range(1, CONV_TAPS):
        sh = pltpu.roll(x, j, 0)
        top = jnp.where(row8 < j, pltpu.roll(halo, j, 0), sh[0:SUBLANES])
        sh = jnp.concatenate([top, sh[SUBLANES:]], axis=0)
        acc = acc + sh * w[CONV_TAPS - 1 - j:CONV_TAPS - j, :]
    return acc


def _resident(shape):
    return pl.BlockSpec(shape, lambda *_: (0,) * len(shape), pipeline_mode=pl.Buffered(1))


def _mod_kernel(c_ref, w_ref, b_ref, o_ref):
    c = c_ref[...]
    a_hi, a_mid, _ = _split3(_silu(c))
    w = w_ref[...]
    w_hi = w.astype(BF16)
    w_lo = (w - w_hi.astype(F32)).astype(BF16)
    o_ref[...] = _dot(a_hi, w_hi) + _dot(a_hi, w_lo) + _dot(a_mid, w_hi) + b_ref[...]


def _modulation(c, ada_w, ada_b):
    depth, d, e = ada_w.shape
    b = c.shape[0]
    tn = 1024
    return pl.pallas_call(
        _mod_kernel,
        grid=(depth, e // tn),
        in_specs=[pl.BlockSpec((b, d), lambda l, j: (0, 0)),
                  pl.BlockSpec((None, d, tn), lambda l, j: (l, 0, j)),
                  pl.BlockSpec((None, 1, tn), lambda l, j: (l, 0, j))],
        out_specs=pl.BlockSpec((None, b, tn), lambda l, j: (l, 0, j)),
        out_shape=jax.ShapeDtypeStruct((depth, b, e), F32),
        compiler_params=_cparams("parallel", "parallel"),
        name="adaln_mod",
    )(c, ada_w, ada_b.reshape(depth, 1, e))


def _rope(y, rc, rs1, rs2):
    reps = y.shape[1] // LANES
    tile = lambda t: jnp.concatenate([t] * reps, axis=1)
    n = y.shape[1]
    half = ROPE_DIM // 2
    return y * tile(rc) + pltpu.roll(y, n - half, 1) * tile(rs1) + pltpu.roll(y, half, 1) * tile(rs2)


def _inproj_kernel(*refs, n_first, per, has_conv, has_dt):
    x_ref, g_ref, sc_ref, sh_ref, w_ref = refs[:5]
    pos = 5
    if has_conv:
        cw_ref, cb_ref = refs[pos:pos + 2]
        pos += 2
    if has_dt:
        wdt_ref = refs[pos]
        pos += 1
    first_ref, second_ref = refs[pos:pos + 2]
    pos += 2
    if has_dt:
        odt_ref = refs[pos]
        pos += 1
    h_scr = refs[pos]
    tm = x_ref.shape[0]
    tn = COL_TILE

    if has_conv:
        halo_scr = refs[pos + 1]

        @pl.when(pl.program_id(0) % per == 0)
        def _():
            halo_scr[...] = jnp.zeros_like(halo_scr)

    h_scr[...] = _norm_mod(x_ref[...], g_ref[...], sc_ref[...], sh_ref[...]).astype(BF16)
    if has_dt:
        odt_ref[...] = _dot(h_scr[...], wdt_ref[...])
    for c0 in range(0, w_ref.shape[1], tn):
        acc = _dot(h_scr[...], w_ref[:, c0:c0 + tn])
        cc = slice(c0 - n_first, c0 - n_first + tn)
        if c0 < n_first:
            first_ref[:, c0:c0 + tn] = acc.astype(BF16)
        elif has_conv:
            second_ref[:, cc] = _causal_conv(acc, halo_scr[:, cc], cw_ref[:, cc], cb_ref[:, cc]).astype(BF16)
            halo_scr[:, cc] = acc[tm - SUBLANES:tm]
        else:
            second_ref[:, cc] = acc.astype(BF16)


def _inproj(x, seq, g, modl, which, w, n_first, conv_w=None, conv_b=None, w_dt=None):
    t, d = x.shape
    n = w.shape[1]
    n_second = n - n_first
    tm = ROW_TILE
    per = seq // tm
    has_conv = conv_w is not None
    in_specs = [pl.BlockSpec((tm, d), lambda i: (i, 0)),
                pl.BlockSpec((1, d), lambda i: (0, 0)),
                pl.BlockSpec((None, 1, d), lambda i: ((i // per) * N_MOD + which + 1, 0, 0)),
                pl.BlockSpec((None, 1, d), lambda i: ((i // per) * N_MOD + which, 0, 0)),
                _resident((d, n))]
    args = [x, g.reshape(1, d), modl, modl, w]
    scratch = [pltpu.VMEM((tm, d), BF16)]
    if has_conv:
        in_specs += [_resident((CONV_TAPS, n_second)), _resident((1, n_second))]
        args += [conv_w, conv_b.reshape(1, n_second)]
        scratch.append(pltpu.VMEM((SUBLANES, n_second), F32))
    out_specs = [pl.BlockSpec((tm, n_first), lambda i: (i, 0)), pl.BlockSpec((tm, n_second), lambda i: (i, 0))]
    out_shape = [jax.ShapeDtypeStruct((t, n_first), BF16), jax.ShapeDtypeStruct((t, n_second), BF16)]
    if w_dt is not None:
        in_specs.append(_resident((d, LANES)))
        args.append(w_dt)
        out_specs.append(pl.BlockSpec((tm, LANES), lambda i: (i, 0)))
        out_shape.append(jax.ShapeDtypeStruct((t, LANES), F32))
    return pl.pallas_call(
        functools.partial(_inproj_kernel, n_first=n_first, per=per, has_conv=has_conv, has_dt=w_dt is not None),
        grid=(t // tm,),
        in_specs=in_specs,
        out_specs=out_specs,
        out_shape=out_shape,
        scratch_shapes=scratch,
        compiler_params=_cparams("arbitrary" if has_conv else "parallel"),
        name="inproj",
    )(*args)


def _inproj_qkv_kernel(x_ref, g_ref, sc_ref, sh_ref, w_ref, rc_ref, rs1_ref, rs2_ref, *refs, dilations,
                       q_scale):
    out_refs = refs[:len(dilations)]
    h_scr, tile_scr = refs[len(dilations):]
    tm = x_ref.shape[0]
    tn = COL_TILE

    h_scr[...] = _norm_mod(x_ref[...], g_ref[...], sc_ref[...], sh_ref[...]).astype(BF16)
    for c0 in range(0, w_ref.shape[1], tn):
        acc = _dot(h_scr[...], w_ref[:, c0:c0 + tn])
        grp, kind = divmod(c0 // tn, 3)
        if kind < 2:
            acc = _rope(acc, rc_ref[...], rs1_ref[...], rs2_ref[...])
            if kind == 0:
                acc = acc * q_scale
        o_ref, d = out_refs[grp], dilations[grp]
        if d == 1:
            o_ref[0, :, kind * tn:(kind + 1) * tn] = acc.astype(BF16)
        else:
            for j in range(tn // LANES):
                tile_scr[j] = acc[:, j * LANES:(j + 1) * LANES]
            for r in range(d):
                for j in range(tn // LANES):
                    o_ref[r, :, kind * tn + j * LANES:kind * tn + (j + 1) * LANES] = (
                        tile_scr[j, pl.ds(r, tm // d, stride=d), :].astype(BF16))


def _inproj_qkv(x, seq, g, modl, which, w, rope, dilations, q_scale):
    t, d = x.shape
    n = w.shape[1]
    tm = ROW_TILE
    per = seq // tm
    batch = t // seq
    gw = n // len(dilations)
    in_specs = [pl.BlockSpec((tm, d), lambda i: (i, 0)),
                pl.BlockSpec((1, d), lambda i: (0, 0)),
                pl.BlockSpec((None, 1, d), lambda i: ((i // per) * N_MOD + which + 1, 0, 0)),
                pl.BlockSpec((None, 1, d), lambda i: ((i // per) * N_MOD + which, 0, 0)),
                _resident((d, n))] + [pl.BlockSpec((tm, LANES), lambda i: (i, 0))] * 3
    return pl.pallas_call(
        functools.partial(_inproj_qkv_kernel, dilations=tuple(dilations), q_scale=q_scale),
        grid=(t // tm,),
        in_specs=in_specs,
        out_specs=[pl.BlockSpec((None, dl, None, tm // dl, gw), lambda i: (i // per, 0, i % per, 0, 0))
                   for dl in dilations],
        out_shape=[jax.ShapeDtypeStruct((batch, dl, per, tm // dl, gw), BF16) for dl in dilations],
        scratch_shapes=[pltpu.VMEM((tm, d), BF16), pltpu.VMEM((COL_TILE // LANES, tm, LANES), F32)],
        compiler_params=_cparams("parallel"),
        name="inproj_qkv",
    )(x, g.reshape(1, d), modl, modl, w, *rope)


def _outproj_kernel(y_ref, z_ref, *refs, n_groups):
    if n_groups:
        ng_ref, *refs = refs
    w_ref, x_ref, gate_ref, gp_ref, o_ref = refs
    y = y_ref[...].astype(F32)
    z = z_ref[...].astype(F32)
    if n_groups:
        y = y * _silu(z)
        gw = y.shape[1] // n_groups
        y = jnp.concatenate([_rms(y[:, g * gw:(g + 1) * gw]) for g in range(n_groups)], axis=1) * ng_ref[...]
    else:
        y = y * jax.nn.gelu(z, approximate=True)
    o_ref[...] = _residual(x_ref[...], _dot(y.astype(BF16), w_ref[...]), gate_ref[...], gp_ref[...])


def _outproj(y, z, w, x, seq, modl, which, g_post, norm_g=None, n_groups=0):
    t, kdim = y.shape
    d = w.shape[1]
    tm = ROW_TILE
    per = seq // tm
    row = pl.BlockSpec((tm, kdim), lambda i: (i, 0))
    in_specs = [row, row]
    args = [y, z]
    if n_groups:
        in_specs.append(pl.BlockSpec((1, kdim), lambda i: (0, 0)))
        args.append(norm_g.reshape(1, kdim))
    in_specs += [_resident((kdim, d)),
                 pl.BlockSpec((tm, d), lambda i: (i, 0)),
                 pl.BlockSpec((None, 1, d), lambda i: ((i // per) * N_MOD + which, 0, 0)),
                 pl.BlockSpec((1, d), lambda i: (0, 0))]
    return pl.pallas_call(
        functools.partial(_outproj_kernel, n_groups=n_groups),
        grid=(t // tm,),
        in_specs=in_specs,
        out_specs=pl.BlockSpec((tm, d), lambda i: (i, 0)),
        out_shape=jax.ShapeDtypeStruct((t, d), F32),
        compiler_params=_cparams("parallel"),
        name="outproj",
    )(*args, w, x, modl, g_post.reshape(1, d))


def _mlp_kernel(x_ref, g_ref, sc_ref, sh_ref, w1_ref, w2_ref, gate_ref, gp_ref, o_ref, h_scr):
    h_scr[...] = _norm_mod(x_ref[...], g_ref[...], sc_ref[...], sh_ref[...]).astype(BF16)
    tf = COL_TILE
    acc = None
    for f0 in range(0, w1_ref.shape[1], tf):
        a = jnp.maximum(_dot(h_scr[...], w1_ref[:, f0:f0 + tf]), 0.0)
        part = _dot((a * a).astype(BF16), w2_ref[f0:f0 + tf, :])
        acc = part if acc is None else acc + part
    o_ref[...] = _residual(x_ref[...], acc, gate_ref[...], gp_ref[...])


def _mlp(x, seq, modl, g_pre, g_post, w1, w2):
    t, d = x.shape
    dff = w1.shape[1]
    tm = ROW_TILE
    per = seq // tm
    mod_spec = lambda which: pl.BlockSpec((None, 1, d), lambda i: ((i // per) * N_MOD + which, 0, 0))
    return pl.pallas_call(
        _mlp_kernel,
        grid=(t // tm,),
        in_specs=[pl.BlockSpec((tm, d), lambda i: (i, 0)),
                  pl.BlockSpec((1, d), lambda i: (0, 0)),
                  mod_spec(4), mod_spec(3),
                  _resident((d, dff)), _resident((dff, d)),
                  mod_spec(5),
                  pl.BlockSpec((1, d), lambda i: (0, 0))],
        out_specs=pl.BlockSpec((tm, d), lambda i: (i, 0)),
        out_shape=jax.ShapeDtypeStruct((t, d), F32),
        scratch_shapes=[pltpu.VMEM((tm, d), BF16)],
        compiler_params=_cparams("parallel"),
        name="mlp",
    )(x, g_pre.reshape(1, d), modl, modl, w1, w2, modl, g_post.reshape(1, d))


def _ssd_decays(dt_raw, dt_bias, a_log_col, n_heads):
    L = dt_raw.shape[0]
    dt_t = _softplus((dt_raw + dt_bias).T[0:n_heads])
    a_col = -jnp.exp(a_log_col[0:n_heads, :])
    row = lax.broadcasted_iota(jnp.int32, (L, L), 0)
    col = lax.broadcasted_iota(jnp.int32, (L, L), 1)
    triu = jnp.where(row <= col, 1.0, 0.0).astype(BF16)
    a_cs_t = sum(_dot(term, triu) for term in _split3(dt_t * a_col))
    e_cs_t = jnp.exp(a_cs_t)
    dte_t = jnp.exp(a_cs_t[:, L - 1:L] - a_cs_t) * dt_t
    to_cols = lambda v: jnp.concatenate([v, jnp.zeros((LANES - n_heads, L), F32)], axis=0).T
    e_hi, e_mid, _ = _split3(to_cols(e_cs_t))
    d_hi, d_mid, _ = _split3(to_cols(dte_t))
    return to_cols(a_cs_t), a_cs_t, dt_t, e_hi, e_mid, d_hi, d_mid


def _ssd_kernel(xbc_ref, dt_ref, dtn_ref, cw_ref, cb_ref, dtb_ref, alog_ref, dsk_ref, exp_ref, y_ref,
                state_scr, tail_scr, act_scr, *decay_scr, d_inner):
    L, N, G = SSD_CHUNK, SSD_D_STATE, SSD_N_GROUPS
    H = d_inner // SSD_HEAD_DIM
    gw = d_inner // G
    conv_dim = d_inner + 2 * G * N
    c = pl.program_id(1)

    @pl.when(c == 0)
    def _():
        state_scr[...] = jnp.zeros_like(state_scr)
        tail_scr[...] = jnp.zeros_like(tail_scr)
        for scr, v in zip(decay_scr, _ssd_decays(dt_ref[...], dtb_ref[...], alog_ref[...], H)):
            scr[...] = v

    a_cs, a_cs_t, dt_t, e_hi, e_mid, d_hi, d_mid = [scr[...] for scr in decay_scr]
    nxt = _ssd_decays(dtn_ref[...], dtb_ref[...], alog_ref[...], H)

    strip = 512
    for s0 in range(0, conv_dim, strip):
        cols = slice(s0, s0 + strip)
        xin = xbc_ref[:, cols].astype(F32)
        act_scr[:, cols] = _silu(_causal_conv(xin, tail_scr[:, cols], cw_ref[:, cols], cb_ref[:, cols]))
        tail_scr[:, cols] = xin[L - SUBLANES:L]

    row = lax.broadcasted_iota(jnp.int32, (L, L), 0)
    col = lax.broadcasted_iota(jnp.int32, (L, L), 1)
    tri = row >= col
    act = lambda lo, hi: act_scr[:, lo:hi]
    lane = lax.broadcasted_iota(jnp.int32, (L, LANES), 1)
    first_head = lane < SSD_HEAD_DIM
    heads_per_pair = LANES // SSD_HEAD_DIM
    for g in range(G):
        gs = slice(g * gw, (g + 1) * gw)
        expand = exp_ref[:, gs]
        e_cs_x = _dot(e_hi, expand) + _dot(e_mid, expand)
        dte_x = _dot(d_hi, expand) + _dot(d_mid, expand)
        xs_g = act(g * gw, (g + 1) * gw)
        bm_g = act(d_inner + g * N, d_inner + (g + 1) * N)
        cm_g = act(d_inner + (G + g) * N, d_inner + (G + g + 1) * N).astype(BF16)
        cb = _dot_nt(cm_g, bm_g.astype(BF16))
        st_prev = state_scr[g]
        y_off = _dot(cm_g, st_prev.astype(BF16)) * e_cs_x
        w_g = (xs_g * dte_x).astype(BF16)
        new_state = _dot(bm_g.T.astype(BF16), w_g)
        state_scr[g] = st_prev * e_cs_x[L - 1:L, :] + new_state
        y_parts = []
        for q in range(gw // LANES):
            xs_p = xs_g[:, q * LANES:(q + 1) * LANES]
            ms = []
            for hh in range(heads_per_pair):
                h = (g * gw + q * LANES) // SSD_HEAD_DIM + hh
                seg = a_cs[:, h:h + 1] - a_cs_t[h:h + 1, :]
                decay = jnp.exp(jnp.where(tri, seg, -jnp.inf))
                ms.append((cb * decay * dt_t[h:h + 1, :]).astype(BF16))
            m_cat = jnp.concatenate(ms, axis=1)
            x_cat = jnp.concatenate([jnp.where(first_head, xs_p, 0.0),
                                     jnp.where(first_head, 0.0, xs_p)], axis=0).astype(BF16)
            y_parts.append(_dot(m_cat, x_cat))
        y = jnp.concatenate(y_parts, axis=1) + y_off + xs_g * dsk_ref[:, gs]
        y_ref[:, gs] = y.astype(y_ref.dtype)

    for scr, v in zip(decay_scr, nxt):
        scr[...] = v


def _ssd_core(xbc, dt_raw, batch, seq, conv_w, conv_b, dt_bias, a_log, d_skip):
    t, conv_dim = xbc.shape
    n_heads = dt_bias.shape[0]
    d_inner = n_heads * SSD_HEAD_DIM
    L = SSD_CHUNK
    nc = seq // L
    pad = lambda v: jnp.pad(v.astype(F32), (0, LANES - n_heads))
    expand = (jnp.arange(LANES)[:, None] == (jnp.arange(d_inner)[None, :] // SSD_HEAD_DIM)).astype(BF16)
    dsk_x = jnp.repeat(d_skip.astype(F32), SSD_HEAD_DIM).reshape(1, d_inner)
    return pl.pallas_call(
        functools.partial(_ssd_kernel, d_inner=d_inner),
        grid=(batch, nc),
        in_specs=[pl.BlockSpec((L, conv_dim), lambda b, c: (b * nc + c, 0)),
                  pl.BlockSpec((L, LANES), lambda b, c: (b * nc + c, 0)),
                  pl.BlockSpec((L, LANES), lambda b, c: (b * nc + jnp.minimum(c + 1, nc - 1), 0)),
                  _resident((CONV_TAPS, conv_dim)), _resident((1, conv_dim)),
                  _resident((1, LANES)), _resident((LANES, 1)), _resident((1, d_inner)),
                  _resident((LANES, d_inner))],
        out_specs=pl.BlockSpec((L, d_inner), lambda b, c: (b * nc + c, 0)),
        out_shape=jax.ShapeDtypeStruct((t, d_inner), BF16),
        scratch_shapes=[pltpu.VMEM((SSD_N_GROUPS, SSD_D_STATE, d_inner // SSD_N_GROUPS), F32),
                        pltpu.VMEM((SUBLANES, conv_dim), F32),
                        pltpu.VMEM((L, conv_dim), F32),
                        pltpu.VMEM((L, LANES), F32), pltpu.VMEM((n_heads, L), F32), pltpu.VMEM((n_heads, L), F32)]
        + [pltpu.VMEM((L, LANES), BF16)] * 4,
        compiler_params=_cparams("parallel", "arbitrary"),
        name="ssd_core",
    )(xbc, dt_raw, dt_raw, conv_w, conv_b.reshape(1, conv_dim), pad(dt_bias).reshape(1, LANES),
      pad(a_log).reshape(LANES, 1), dsk_x, expand)


def _lru_kernel(xc_ref, wga_ref, bga_ref, wgx_ref, bgx_ref, lam_ref, y_ref, h_scr, a_scr, u_scr):
    tc, width = y_ref.shape
    blk = width // LRU_N_BLOCKS

    @pl.when(pl.program_id(1) == 0)
    def _():
        h_scr[...] = jnp.zeros_like(h_scr)

    neg_sp = _softplus(-lam_ref[...])
    for k in range(LRU_N_BLOCKS):
        sl = slice(k * blk, (k + 1) * blk)
        xb16 = xc_ref[:, sl]
        r = _sigmoid(_dot(xb16, wga_ref[k]) + bga_ref[:, sl])
        i = _sigmoid(_dot(xb16, wgx_ref[k]) + bgx_ref[:, sl])
        log_a = -LRU_C * r * neg_sp[:, sl]
        a_scr[:, sl] = jnp.exp(log_a)
        u_scr[:, sl] = jnp.sqrt(1.0 - jnp.exp(2.0 * log_a)) * (i * xb16.astype(F32))

    row = lax.broadcasted_iota(jnp.int32, (SUBLANES, width), 0)

    def body(t, h):
        rows = pl.ds(pl.multiple_of(t * SUBLANES, SUBLANES), SUBLANES)
        a = a_scr[rows, :]
        u = u_scr[rows, :]
        for k in (1, 2, 4):
            keep = row >= k
            a_sh = jnp.where(keep, pltpu.roll(a, k, 0), 1.0)
            u_sh = jnp.where(keep, pltpu.roll(u, k, 0), 0.0)
            u = a * u_sh + u
            a = a * a_sh
        hh = a * h + u
        u_scr[rows, :] = hh
        return jnp.broadcast_to(hh[SUBLANES - 1:SUBLANES, :], hh.shape)

    h_scr[...] = lax.fori_loop(0, tc // SUBLANES, body, h_scr[...])
    y_ref[...] = u_scr[...].astype(y_ref.dtype)


def _lru_core(xc, batch, seq, w_ga, b_ga, w_gx, b_gx, lam):
    t, width = xc.shape
    tc = LRU_CHUNK
    nc = seq // tc
    vec = lambda v: v.astype(F32).reshape(1, width)
    return pl.pallas_call(
        _lru_kernel,
        grid=(batch, nc),
        in_specs=[pl.BlockSpec((tc, width), lambda b, c: (b * nc + c, 0)),
                  _resident(w_ga.shape), _resident((1, width)), _resident(w_gx.shape),
                  _resident((1, width)), _resident((1, width))],
        out_specs=pl.BlockSpec((tc, width), lambda b, c: (b * nc + c, 0)),
        out_shape=jax.ShapeDtypeStruct((t, width), BF16),
        scratch_shapes=[pltpu.VMEM((SUBLANES, width), F32),
                        pltpu.VMEM((tc, width), F32), pltpu.VMEM((tc, width), F32)],
        compiler_params=_cparams("parallel", "arbitrary"),
        name="lru_core",
    )(xc, w_ga.astype(BF16), vec(b_ga), w_gx.astype(BF16), vec(b_gx), vec(lam))


def _attn_kernel(q_ref, k_ref, v_ref, o_ref, lse_ref, kp_scr, vp_scr, *tile_scr, dilation):
    L, width = q_ref.shape[1:]
    n = pl.program_id(1)

    @pl.when(n == 0)
    def _():
        kp_scr[...] = jnp.zeros_like(kp_scr)
        vp_scr[...] = jnp.zeros_like(vp_scr)

    qi = lax.broadcasted_iota(jnp.int32, (L, 2 * L), 0)
    ki = lax.broadcasted_iota(jnp.int32, (L, 2 * L), 1)
    dist = qi + L - ki
    valid = (dist >= 0) & (dist <= L) & (ki + (n - 1) * L >= 0)
    lane_q = lax.broadcasted_iota(jnp.int32, (L, LANES), 1)
    lane_v = lax.broadcasted_iota(jnp.int32, (2 * L, LANES), 1)
    zero = jnp.zeros((), BF16)
    heads_per_pair = LANES // ATTN_HEAD_DIM

    def residue(r, carry):
        lse_tile = jnp.zeros((L, LANES), F32)
        o_parts = []
        for j in range(width // LANES):
            sl = slice(j * LANES, (j + 1) * LANES)
            qp = q_ref[r, :, sl]
            kk = jnp.concatenate([kp_scr[r, :, sl], k_ref[r, :, sl]], axis=0)
            vv = jnp.concatenate([vp_scr[r, :, sl], v_ref[r, :, sl]], axis=0)
            acc = jnp.zeros((L, LANES), F32)
            for hh in range(heads_per_pair):
                in_head_q = lane_q < ATTN_HEAD_DIM if hh == 0 else lane_q >= ATTN_HEAD_DIM
                in_head_v = lane_v < ATTN_HEAD_DIM if hh == 0 else lane_v >= ATTN_HEAD_DIM
                s = _dot_nt(jnp.where(in_head_q, qp, zero), kk)
                s = jnp.where(valid, s, MASK_VALUE)
                m = jnp.max(s, axis=-1, keepdims=True)
                p = jnp.exp(s - m)
                l = jnp.sum(p, axis=-1, keepdims=True)
                o = _dot(p.astype(BF16), jnp.where(in_head_v, vv, zero))
                acc = acc + o / l
                lse_tile = jnp.where(lane_q == j * heads_per_pair + hh, m + jnp.log(l), lse_tile)
            o_parts.append(acc)
        if dilation == 1:
            o_ref[...] = jnp.concatenate(o_parts, axis=1).astype(o_ref.dtype)
            lse_ref[...] = lse_tile
        else:
            o_scr, lse_scr = tile_scr
            for j, part in enumerate(o_parts):
                o_scr[j, pl.ds(r, L, stride=dilation), :] = part
            lse_scr[pl.ds(r, L, stride=dilation), :] = lse_tile
        kp_scr[r] = k_ref[r]
        vp_scr[r] = v_ref[r]
        return carry

    if dilation == 1:
        residue(0, 0)
    else:
        lax.fori_loop(0, dilation, residue, 0)
        o_scr, lse_scr = tile_scr
        for j in range(width // LANES):
            o_ref[:, j * LANES:(j + 1) * LANES] = o_scr[j].astype(o_ref.dtype)
        lse_ref[...] = lse_scr[...]


def _attn_group(qkv_g, batch, seq, dilation, span):
    width = ATTN_HEADS * ATTN_HEAD_DIM
    m = seq // dilation
    nblk = m // span
    t = batch * seq
    rows = span * dilation
    spec = lambda which: pl.BlockSpec((None, dilation, span, width), lambda b, n: (b, 0, n, which))
    scratch = [pltpu.VMEM((dilation, span, width), BF16), pltpu.VMEM((dilation, span, width), BF16)]
    if dilation > 1:
        scratch += [pltpu.VMEM((width // LANES, rows, LANES), F32), pltpu.VMEM((rows, LANES), F32)]
    return pl.pallas_call(
        functools.partial(_attn_kernel, dilation=dilation),
        grid=(batch, nblk),
        in_specs=[spec(0), spec(1), spec(2)],
        out_specs=[pl.BlockSpec((rows, width), lambda b, n: (b * nblk + n, 0)),
                   pl.BlockSpec((rows, LANES), lambda b, n: (b * nblk + n, 0))],
        out_shape=[jax.ShapeDtypeStruct((t, width), BF16),
                   jax.ShapeDtypeStruct((t, LANES), F32)],
        scratch_shapes=scratch,
        compiler_params=_cparams("parallel", "arbitrary"),
        name=f"attn_d{dilation}",
    )(qkv_g, qkv_g, qkv_g)


def _attn_out_kernel(o0_ref, o1_ref, o2_ref, l0_ref, l1_ref, l2_ref, w_ref, x_ref, gate_ref, gp_ref,
                     out_ref):
    tm, width = o0_ref.shape
    lane = lax.broadcasted_iota(jnp.int32, (tm, LANES), 1)
    first_head = lane < ATTN_HEAD_DIM
    lses = (l0_ref[...], l1_ref[...], l2_ref[...])
    outs = (o0_ref, o1_ref, o2_ref)
    parts = []
    for j in range(width // LANES):
        sl = slice(j * LANES, (j + 1) * LANES)
        ls = [jnp.where(first_head, l[:, 2 * j:2 * j + 1], l[:, 2 * j + 1:2 * j + 2]) for l in lses]
        mx = jnp.maximum(jnp.maximum(ls[0], ls[1]), ls[2])
        es = [jnp.exp(l - mx) for l in ls]
        num = sum(e * o[:, sl].astype(F32) for e, o in zip(es, outs))
        parts.append(num / (es[0] + es[1] + es[2]))
    o = jnp.concatenate(parts, axis=1).astype(BF16)
    out_ref[...] = _residual(x_ref[...], _dot(o, w_ref[...]), gate_ref[...], gp_ref[...])


def _attn_out(outs, lses, w, x, seq, modl, which, g_post):
    t, d = x.shape
    width = w.shape[0]
    tm = ROW_TILE
    per = seq // tm
    row = lambda n: pl.BlockSpec((tm, n), lambda i: (i, 0))
    return pl.pallas_call(
        _attn_out_kernel,
        grid=(t // tm,),
        in_specs=[row(width)] * 3 + [row(LANES)] * 3 + [
            _resident((width, d)),
            row(d),
            pl.BlockSpec((None, 1, d), lambda i: ((i // per) * N_MOD + which, 0, 0)),
            pl.BlockSpec((1, d), lambda i: (0, 0))],
        out_specs=row(d),
        out_shape=jax.ShapeDtypeStruct((t, d), F32),
        compiler_params=_cparams("parallel"),
        name="attn_out",
    )(*outs, *lses, w, x, modl, g_post.reshape(1, d))


def _rope_tables(positions):
    half = ROPE_DIM // 2
    inv_freq = ROPE_THETA ** (-jnp.arange(0, ROPE_DIM, 2, dtype=F32) / ROPE_DIM)
    ang = positions.astype(F32).reshape(-1, 1) * inv_freq
    cos, sin = jnp.cos(ang), jnp.sin(ang)
    t = cos.shape[0]
    rest = ATTN_HEAD_DIM - ROPE_DIM
    zeros = jnp.zeros((t, half), F32)
    c = jnp.concatenate([cos, cos, jnp.ones((t, rest), F32)], axis=1)
    s1 = jnp.concatenate([-sin, zeros, jnp.zeros((t, rest), F32)], axis=1)
    s2 = jnp.concatenate([zeros, sin, jnp.zeros((t, rest), F32)], axis=1)
    reps = LANES // ATTN_HEAD_DIM
    return tuple(jnp.tile(v, (1, reps)) for v in (c, s1, s2))


def _attention_layer(xf, batch, seq, g_pre, g_post, modl, rope, w_qkv, w_out):
    dilations = [d for _, d in ATTN_CONFIGS]
    qkvs = _inproj_qkv(xf, seq, g_pre, modl, 0, w_qkv, rope, dilations, ATTN_HEAD_DIM ** -0.5)
    outs, lses = [], []
    for qkv_g, (window, dilation) in zip(qkvs, ATTN_CONFIGS):
        qkv_g = qkv_g.reshape(batch, dilation, seq // dilation, qkv_g.shape[-1])
        o, lse = _attn_group(qkv_g, batch, seq, dilation, window // dilation)
        outs.append(o)
        lses.append(lse)
    return _attn_out(outs, lses, w_out, xf, seq, modl, 2, g_post)


def kernel(x, c, positions, ada_w, ada_b, norm_mix_pre, norm_mix_post, norm_mlp_pre, norm_mlp_post, mlp_w1, mlp_w2, ssd_w_in, ssd_conv_w, ssd_conv_b, ssd_dt_bias, ssd_a_log, ssd_d, ssd_norm, ssd_w_out, lru_w_in, lru_conv_w, lru_conv_b, lru_w_gate_a, lru_b_gate_a, lru_w_gate_x, lru_b_gate_x, lru_lambda, lru_w_out, attn_w_qkv, attn_w_out):
    batch, seq, d = x.shape
    depth = ada_w.shape[0]
    xf = x.reshape(batch * seq, d)
    mod = _modulation(c, ada_w, ada_b)
    rope = _rope_tables(positions)
    for layer in range(depth):
        modl = mod[layer].reshape(batch * N_MOD, 1, d)
        kind, occ = layer % N_MIXERS, layer // N_MIXERS
        if kind == 0:
            w_in = ssd_w_in[occ]
            n_heads = ssd_dt_bias.shape[1]
            d_inner = n_heads * SSD_HEAD_DIM
            n_main = w_in.shape[1] - n_heads
            w_dt = jnp.pad(w_in[:, n_main:], ((0, 0), (0, LANES - n_heads))).astype(BF16)
            z, xbc, dt_raw = _inproj(xf, seq, norm_mix_pre[layer], modl, 0, w_in[:, :n_main].astype(BF16),
                                     d_inner, w_dt=w_dt)
            y = _ssd_core(xbc, dt_raw, batch, seq, ssd_conv_w[occ], ssd_conv_b[occ], ssd_dt_bias[occ],
                          ssd_a_log[occ], ssd_d[occ])
            xf = _outproj(y, z, ssd_w_out[occ].astype(BF16), xf, seq, modl, 2, norm_mix_post[layer],
                          norm_g=ssd_norm[occ], n_groups=SSD_N_GROUPS)
        elif kind == 1:
            width = lru_conv_w.shape[-1]
            gate, xc = _inproj(xf, seq, norm_mix_pre[layer], modl, 0, lru_w_in[occ].astype(BF16), width,
                               conv_w=lru_conv_w[occ], conv_b=lru_conv_b[occ])
            hs = _lru_core(xc, batch, seq, lru_w_gate_a[occ], lru_b_gate_a[occ], lru_w_gate_x[occ],
                           lru_b_gate_x[occ], lru_lambda[occ])
            xf = _outproj(hs, gate, lru_w_out[occ].astype(BF16), xf, seq, modl, 2, norm_mix_post[layer])
        else:
            xf = _attention_layer(xf, batch, seq, norm_mix_pre[layer], norm_mix_post[layer], modl, rope,
                                  attn_w_qkv[occ].astype(BF16), attn_w_out[occ].astype(BF16))
        xf = _mlp(xf, seq, modl, norm_mlp_pre[layer], norm_mlp_post[layer],
                  mlp_w1[layer].astype(BF16), mlp_w2[layer].astype(BF16))
    return xf.reshape(batch, seq, d)
```

```python
import functools

import jax
import jax.numpy as jnp
from jax import lax
from jax.experimental import pallas as pl
from jax.experimental.pallas import tpu as pltpu

F32 = jnp.float32
BF16 = jnp.bfloat16

NORM_EPS = 1e-6
N_MIXERS = 3
N_MOD = 6

SSD_HEAD_DIM = 64
SSD_N_GROUPS = 8
SSD_D_STATE = 128
SSD_CHUNK = 128

LRU_N_BLOCKS = 4
LRU_C = 8.0
LRU_CHUNK = 256

ATTN_HEAD_DIM = 64
ATTN_HEADS = 8
ATTN_CONFIGS = ((128, 1), (512, 4), (2048, 16))
ROPE_THETA = 500000.0
ROPE_DIM = ATTN_HEAD_DIM // 4
MASK_VALUE = -1e30

LANES = 128
SUBLANES = 8
VMEM_LIMIT = 56 * 1024 * 1024

ROW_TILE = 512
COL_TILE = 512


def _cparams(*sem):
    return pltpu.CompilerParams(dimension_semantics=sem, vmem_limit_bytes=VMEM_LIMIT)


def _dot(a, b):
    return jnp.dot(a, b, preferred_element_type=F32)


def _dot_nt(a, b):
    return lax.dot_general(a, b, (((1,), (1,)), ((), ())), preferred_element_type=F32)


def _split3(v):
    hi = v.astype(BF16)
    r1 = v - hi.astype(F32)
    mid = r1.astype(BF16)
    lo = (r1 - mid.astype(F32)).astype(BF16)
    return hi, mid, lo


def _rms(y):
    return y * lax.rsqrt(jnp.mean(y * y, axis=-1, keepdims=True) + NORM_EPS)


def _norm_mod(x, g, sc, sh):
    return _rms(x) * g * (1.0 + sc) + sh


def _residual(x, y, gate, g_post):
    return x + (1.0 + gate) * (_rms(y) * g_post)


def _softplus(x):
    return jnp.maximum(x, 0.0) + jnp.log1p(jnp.exp(-jnp.abs(x)))


def _sigmoid(x):
    return 0.5 + 0.5 * jnp.tanh(0.5 * x)


def _silu(x):
    h = 0.5 * x
    return h + h * jnp.tanh(h)


CONV_TAPS = 4


PERM_CHUNK = SSD_CHUNK
PERM_SEG = PERM_CHUNK // SUBLANES
HALO_ROWS = (CONV_TAPS - 1) * SUBLANES


def _perm_time(r):
    return (r % SUBLANES) * PERM_SEG + r // SUBLANES


def _causal_conv(x, halo, w, b):
    n = PERM_CHUNK
    row8 = lax.broadcasted_iota(jnp.int32, (SUBLANES, x.shape[1]), 0)
    wrapped = []
    for k in range(CONV_TAPS - 1):
        cur = x[n - HALO_ROWS + k * SUBLANES:n - HALO_ROWS + (k + 1) * SUBLANES]
        prev = halo[k * SUBLANES:(k + 1) * SUBLANES]
        wrapped.append(jnp.where(row8 == 0, pltpu.roll(prev, 1, 0), pltpu.roll(cur, 1, 0)))
    acc = x * w[CONV_TAPS - 1:CONV_TAPS, :] + b
    for j in range(1, CONV_TAPS):
        xj = jnp.concatenate(wrapped[CONV_TAPS - 1 - j:] + [x[0:n - j * SUBLANES]], axis=0)
        acc = acc + xj * w[CONV_TAPS - 1 - j:CONV_TAPS - j, :]
    return acc


def _permute_rows(v, scr, to_time_order):
    tm = v.shape[0]
    nb = v.shape[1] // LANES
    for j in range(nb):
        scr[j] = v[:, j * LANES:(j + 1) * LANES]
    cols = []
    for j in range(nb):
        rows = []
        for c0 in range(0, tm, PERM_CHUNK):
            for a in range(PERM_CHUNK // SUBLANES):
                if to_time_order:
                    s, i0 = divmod(a * SUBLANES, PERM_SEG)
                    rows.append(scr[j, pl.ds(c0 + i0 * SUBLANES + s, SUBLANES, stride=SUBLANES), :])
                else:
                    rows.append(scr[j, pl.ds(c0 + a, SUBLANES, stride=PERM_SEG), :])
        cols.append(jnp.concatenate(rows, axis=0))
    return jnp.concatenate(cols, axis=1)


def _resident(shape):
    return pl.BlockSpec(shape, lambda *_: (0,) * len(shape), pipeline_mode=pl.Buffered(1))


def _mod_kernel(c_ref, w_ref, b_ref, o_ref):
    c = c_ref[...]
    a_hi, a_mid, _ = _split3(_silu(c))
    w = w_ref[...]
    w_hi = w.astype(BF16)
    w_lo = (w - w_hi.astype(F32)).astype(BF16)
    o_ref[...] = _dot(a_hi, w_hi) + _dot(a_hi, w_lo) + _dot(a_mid, w_hi) + b_ref[...]


def _modulation(c, ada_w, ada_b):
    depth, d, e = ada_w.shape
    b = c.shape[0]
    tn = 1024
    return pl.pallas_call(
        _mod_kernel,
        grid=(depth, e // tn),
        in_specs=[pl.BlockSpec((b, d), lambda l, j: (0, 0)),
                  pl.BlockSpec((None, d, tn), lambda l, j: (l, 0, j)),
                  pl.BlockSpec((None, 1, tn), lambda l, j: (l, 0, j))],
        out_specs=pl.BlockSpec((None, b, tn), lambda l, j: (l, 0, j)),
        out_shape=jax.ShapeDtypeStruct((depth, b, e), F32),
        compiler_params=_cparams("parallel", "parallel"),
        name="adaln_mod",
    )(c, ada_w, ada_b.reshape(depth, 1, e))


def _rope(y, rc, rs1, rs2):
    reps = y.shape[1] // LANES
    tile = lambda t: jnp.concatenate([t] * reps, axis=1)
    n = y.shape[1]
    half = ROPE_DIM // 2
    return y * tile(rc) + pltpu.roll(y, n - half, 1) * tile(rs1) + pltpu.roll(y, half, 1) * tile(rs2)


def _inproj_kernel(*refs, n_first, per, has_conv, has_dt):
    x_ref, g_ref, sc_ref, sh_ref, w_ref = refs[:5]
    pos = 5
    if has_conv:
        cw_ref, cb_ref = refs[pos:pos + 2]
        pos += 2
    if has_dt:
        wdt_ref = refs[pos]
        pos += 1
    first_ref, second_ref = refs[pos:pos + 2]
    pos += 2
    if has_dt:
        odt_ref = refs[pos]
        pos += 1
    h_scr, perm_scr = refs[pos:pos + 2]
    tm = x_ref.shape[0]
    tn = COL_TILE

    if has_conv:
        halo_scr = refs[pos + 2]

        @pl.when(pl.program_id(0) % per == 0)
        def _():
            halo_scr[...] = jnp.zeros_like(halo_scr)

    h = _norm_mod(x_ref[...], g_ref[...], sc_ref[...], sh_ref[...])
    h_scr[...] = _permute_rows(h, perm_scr, to_time_order=False).astype(BF16)
    if has_dt:
        odt_ref[...] = _dot(h_scr[...], wdt_ref[...])
    for c0 in range(0, w_ref.shape[1], tn):
        acc = _dot(h_scr[...], w_ref[:, c0:c0 + tn])
        cc = slice(c0 - n_first, c0 - n_first + tn)
        if c0 < n_first:
            first_ref[:, c0:c0 + tn] = acc.astype(BF16)
        elif has_conv:
            halo = halo_scr[:, cc]
            for r0 in range(0, tm, PERM_CHUNK):
                x_c = acc[r0:r0 + PERM_CHUNK]
                second_ref[r0:r0 + PERM_CHUNK, cc] = _causal_conv(x_c, halo, cw_ref[:, cc], cb_ref[:, cc]).astype(BF16)
                halo = x_c[PERM_CHUNK - HALO_ROWS:]
            halo_scr[:, cc] = halo
        else:
            second_ref[:, cc] = acc.astype(BF16)


def _inproj(x, seq, g, modl, which, w, n_first, conv_w=None, conv_b=None, w_dt=None):
    t, d = x.shape
    n = w.shape[1]
    n_second = n - n_first
    tm = ROW_TILE
    per = seq // tm
    has_conv = conv_w is not None
    in_specs = [pl.BlockSpec((tm, d), lambda i: (i, 0)),
                pl.BlockSpec((1, d), lambda i: (0, 0)),
                pl.BlockSpec((None, 1, d), lambda i: ((i // per) * N_MOD + which + 1, 0, 0)),
                pl.BlockSpec((None, 1, d), lambda i: ((i // per) * N_MOD + which, 0, 0)),
                _resident((d, n))]
    args = [x, g.reshape(1, d), modl, modl, w]
    scratch = [pltpu.VMEM((tm, d), BF16), pltpu.VMEM((d // LANES, tm, LANES), F32)]
    if has_conv:
        in_specs += [_resident((CONV_TAPS, n_second)), _resident((1, n_second))]
        args += [conv_w, conv_b.reshape(1, n_second)]
        scratch.append(pltpu.VMEM((HALO_ROWS, n_second), F32))
    out_specs = [pl.BlockSpec((tm, n_first), lambda i: (i, 0)), pl.BlockSpec((tm, n_second), lambda i: (i, 0))]
    out_shape = [jax.ShapeDtypeStruct((t, n_first), BF16), jax.ShapeDtypeStruct((t, n_second), BF16)]
    if w_dt is not None:
        in_specs.append(_resident((d, LANES)))
        args.append(w_dt)
        out_specs.append(pl.BlockSpec((tm, LANES), lambda i: (i, 0)))
        out_shape.append(jax.ShapeDtypeStruct((t, LANES), F32))
    return pl.pallas_call(
        functools.partial(_inproj_kernel, n_first=n_first, per=per, has_conv=has_conv, has_dt=w_dt is not None),
        grid=(t // tm,),
        in_specs=in_specs,
        out_specs=out_specs,
        out_shape=out_shape,
        scratch_shapes=scratch,
        compiler_params=_cparams("arbitrary" if has_conv else "parallel"),
        name="inproj",
    )(*args)


def _inproj_qkv_kernel(x_ref, g_ref, sc_ref, sh_ref, w_ref, rc_ref, rs1_ref, rs2_ref, *refs, dilations,
                       q_scale):
    out_refs = refs[:len(dilations)]
    h_scr, tile_scr = refs[len(dilations):]
    tm = x_ref.shape[0]
    tn = COL_TILE

    h_scr[...] = _norm_mod(x_ref[...], g_ref[...], sc_ref[...], sh_ref[...]).astype(BF16)
    for c0 in range(0, w_ref.shape[1], tn):
        acc = _dot(h_scr[...], w_ref[:, c0:c0 + tn])
        grp, kind = divmod(c0 // tn, 3)
        if kind < 2:
            acc = _rope(acc, rc_ref[...], rs1_ref[...], rs2_ref[...])
            if kind == 0:
                acc = acc * q_scale
        o_ref, d = out_refs[grp], dilations[grp]
        if d == 1:
            o_ref[0, :, kind * tn:(kind + 1) * tn] = acc.astype(BF16)
        else:
            for j in range(tn // LANES):
                tile_scr[j] = acc[:, j * LANES:(j + 1) * LANES]
            for r in range(d):
                for j in range(tn // LANES):
                    o_ref[r, :, kind * tn + j * LANES:kind * tn + (j + 1) * LANES] = (
                        tile_scr[j, pl.ds(r, tm // d, stride=d), :].astype(BF16))


def _inproj_qkv(x, seq, g, modl, which, w, rope, dilations, q_scale):
    t, d = x.shape
    n = w.shape[1]
    tm = ROW_TILE
    per = seq // tm
    batch = t // seq
    gw = n // len(dilations)
    in_specs = [pl.BlockSpec((tm, d), lambda i: (i, 0)),
                pl.BlockSpec((1, d), lambda i: (0, 0)),
                pl.BlockSpec((None, 1, d), lambda i: ((i // per) * N_MOD + which + 1, 0, 0)),
                pl.BlockSpec((None, 1, d), lambda i: ((i // per) * N_MOD + which, 0, 0)),
                _resident((d, n))] + [pl.BlockSpec((tm, LANES), lambda i: (i, 0))] * 3
    return pl.pallas_call(
        functools.partial(_inproj_qkv_kernel, dilations=tuple(dilations), q_scale=q_scale),
        grid=(t // tm,),
        in_specs=in_specs,
        out_specs=[pl.BlockSpec((None, dl, None, tm // dl, gw), lambda i: (i // per, 0, i % per, 0, 0))
                   for dl in dilations],
        out_shape=[jax.ShapeDtypeStruct((batch, dl, per, tm // dl, gw), BF16) for dl in dilations],
        scratch_shapes=[pltpu.VMEM((tm, d), BF16), pltpu.VMEM((COL_TILE // LANES, tm, LANES), F32)],
        compiler_params=_cparams("parallel"),
        name="inproj_qkv",
    )(x, g.reshape(1, d), modl, modl, w, *rope)


def _outproj_kernel(y_ref, z_ref, *refs, n_groups):
    if n_groups:
        ng_ref, *refs = refs
    w_ref, x_ref, gate_ref, gp_ref, o_ref, perm_scr = refs
    y = y_ref[...].astype(F32)
    z = z_ref[...].astype(F32)
    if n_groups:
        y = y * _silu(z)
        gw = y.shape[1] // n_groups
        y = jnp.concatenate([_rms(y[:, g * gw:(g + 1) * gw]) for g in range(n_groups)], axis=1) * ng_ref[...]
    else:
        y = y * jax.nn.gelu(z, approximate=True)
    out = _permute_rows(_dot(y.astype(BF16), w_ref[...]), perm_scr, to_time_order=True)
    o_ref[...] = _residual(x_ref[...], out, gate_ref[...], gp_ref[...])


def _outproj(y, z, w, x, seq, modl, which, g_post, norm_g=None, n_groups=0):
    t, kdim = y.shape
    d = w.shape[1]
    tm = ROW_TILE
    per = seq // tm
    row = pl.BlockSpec((tm, kdim), lambda i: (i, 0))
    in_specs = [row, row]
    args = [y, z]
    if n_groups:
        in_specs.append(pl.BlockSpec((1, kdim), lambda i: (0, 0)))
        args.append(norm_g.reshape(1, kdim))
    in_specs += [_resident((kdim, d)),
                 pl.BlockSpec((tm, d), lambda i: (i, 0)),
                 pl.BlockSpec((None, 1, d), lambda i: ((i // per) * N_MOD + which, 0, 0)),
                 pl.BlockSpec((1, d), lambda i: (0, 0))]
    return pl.pallas_call(
        functools.partial(_outproj_kernel, n_groups=n_groups),
        grid=(t // tm,),
        in_specs=in_specs,
        out_specs=pl.BlockSpec((tm, d), lambda i: (i, 0)),
        out_shape=jax.ShapeDtypeStruct((t, d), F32),
        scratch_shapes=[pltpu.VMEM((d // LANES, tm, LANES), F32)],
        compiler_params=_cparams("parallel"),
        name="outproj",
    )(*args, w, x, modl, g_post.reshape(1, d))


def _mlp_kernel(x_ref, g_ref, sc_ref, sh_ref, w1_ref, w2_ref, gate_ref, gp_ref, o_ref, h_scr):
    h_scr[...] = _norm_mod(x_ref[...], g_ref[...], sc_ref[...], sh_ref[...]).astype(BF16)
    tf = COL_TILE
    acc = None
    for f0 in range(0, w1_ref.shape[1], tf):
        a = jnp.maximum(_dot(h_scr[...], w1_ref[:, f0:f0 + tf]), 0.0)
        part = _dot((a * a).astype(BF16), w2_ref[f0:f0 + tf, :])
        acc = part if acc is None else acc + part
    o_ref[...] = _residual(x_ref[...], acc, gate_ref[...], gp_ref[...])


def _mlp(x, seq, modl, g_pre, g_post, w1, w2, layer):
    t, d = x.shape
    dff = w1.shape[2]
    tm = ROW_TILE
    per = seq // tm
    mod_spec = lambda which: pl.BlockSpec((None, 1, d), lambda i: ((i // per) * N_MOD + which, 0, 0))
    slab = lambda r, c: pl.BlockSpec((None, r, c), lambda i: (layer, 0, 0), pipeline_mode=pl.Buffered(1))
    return pl.pallas_call(
        _mlp_kernel,
        grid=(t // tm,),
        in_specs=[pl.BlockSpec((tm, d), lambda i: (i, 0)),
                  pl.BlockSpec((1, d), lambda i: (0, 0)),
                  mod_spec(4), mod_spec(3),
                  slab(d, dff), slab(dff, d),
                  mod_spec(5),
                  pl.BlockSpec((1, d), lambda i: (0, 0))],
        out_specs=pl.BlockSpec((tm, d), lambda i: (i, 0)),
        out_shape=jax.ShapeDtypeStruct((t, d), F32),
        scratch_shapes=[pltpu.VMEM((tm, d), BF16)],
        compiler_params=_cparams("parallel"),
        name="mlp",
    )(x, g_pre.reshape(1, d), modl, modl, w1, w2, modl, g_post.reshape(1, d))


def _ssd_decays(dt_raw, dt_bias, a_log_col, n_heads, cum):
    L = dt_raw.shape[0]
    dt_t = _softplus((dt_raw + dt_bias).T[0:n_heads])
    a_col = -jnp.exp(a_log_col[0:n_heads, :])
    a_cs_t = sum(_dot(term, cum) for term in _split3(dt_t * a_col))
    e_cs_t = jnp.exp(a_cs_t)
    dte_t = jnp.exp(a_cs_t[:, L - 1:L] - a_cs_t) * dt_t
    to_cols = lambda v: jnp.concatenate([v, jnp.zeros((LANES - n_heads, L), F32)], axis=0).T
    e_hi, e_mid, _ = _split3(to_cols(e_cs_t))
    d_hi, d_mid, _ = _split3(to_cols(dte_t))
    return to_cols(a_cs_t), a_cs_t, dt_t, e_hi, e_mid, d_hi, d_mid


def _ssd_kernel(xbc_ref, dt_ref, dtn_ref, cum_ref, cw_ref, cb_ref, dtb_ref, alog_ref, dsk_ref, exp_ref, y_ref,
                state_scr, tail_scr, act_scr, *decay_scr, d_inner):
    L, N, G = SSD_CHUNK, SSD_D_STATE, SSD_N_GROUPS
    H = d_inner // SSD_HEAD_DIM
    gw = d_inner // G
    conv_dim = d_inner + 2 * G * N
    c = pl.program_id(1)

    @pl.when(c == 0)
    def _():
        state_scr[...] = jnp.zeros_like(state_scr)
        tail_scr[...] = jnp.zeros_like(tail_scr)
        for scr, v in zip(decay_scr, _ssd_decays(dt_ref[...], dtb_ref[...], alog_ref[...], H, cum_ref[...])):
            scr[...] = v

    a_cs, a_cs_t, dt_t, e_hi, e_mid, d_hi, d_mid = [scr[...] for scr in decay_scr]
    nxt = _ssd_decays(dtn_ref[...], dtb_ref[...], alog_ref[...], H, cum_ref[...])

    strip = 512
    for s0 in range(0, conv_dim, strip):
        cols = slice(s0, s0 + strip)
        xin = xbc_ref[:, cols].astype(F32)
        act_scr[:, cols] = _silu(_causal_conv(xin, tail_scr[:, cols], cw_ref[:, cols], cb_ref[:, cols]))
        tail_scr[:, cols] = xin[L - HALO_ROWS:L]

    row = lax.broadcasted_iota(jnp.int32, (L, L), 0)
    col = lax.broadcasted_iota(jnp.int32, (L, L), 1)
    tri = _perm_time(row) >= _perm_time(col)
    act = lambda lo, hi: act_scr[:, lo:hi]
    lane = lax.broadcasted_iota(jnp.int32, (L, LANES), 1)
    first_head = lane < SSD_HEAD_DIM
    heads_per_pair = LANES // SSD_HEAD_DIM
    for g in range(G):
        gs = slice(g * gw, (g + 1) * gw)
        expand = exp_ref[:, gs]
        e_cs_x = _dot(e_hi, expand) + _dot(e_mid, expand)
        dte_x = _dot(d_hi, expand) + _dot(d_mid, expand)
        xs_g = act(g * gw, (g + 1) * gw)
        bm_g = act(d_inner + g * N, d_inner + (g + 1) * N)
        cm_g = act(d_inner + (G + g) * N, d_inner + (G + g + 1) * N).astype(BF16)
        cb = _dot_nt(cm_g, bm_g.astype(BF16))
        st_prev = state_scr[g]
        y_off = _dot(cm_g, st_prev.astype(BF16)) * e_cs_x
        w_g = (xs_g * dte_x).astype(BF16)
        new_state = _dot(bm_g.T.astype(BF16), w_g)
        state_scr[g] = st_prev * e_cs_x[L - 1:L, :] + new_state
        y_parts = []
        for q in range(gw // LANES):
            xs_p = xs_g[:, q * LANES:(q + 1) * LANES]
            ms = []
            for hh in range(heads_per_pair):
                h = (g * gw + q * LANES) // SSD_HEAD_DIM + hh
                seg = a_cs[:, h:h + 1] - a_cs_t[h:h + 1, :]
                decay = jnp.exp(jnp.where(tri, seg, -jnp.inf))
                ms.append((cb * decay * dt_t[h:h + 1, :]).astype(BF16))
            m_cat = jnp.concatenate(ms, axis=1)
            x_cat = jnp.concatenate([jnp.where(first_head, xs_p, 0.0),
                                     jnp.where(first_head, 0.0, xs_p)], axis=0).astype(BF16)
            y_parts.append(_dot(m_cat, x_cat))
        y = jnp.concatenate(y_parts, axis=1) + y_off + xs_g * dsk_ref[:, gs]
        y_ref[:, gs] = y.astype(y_ref.dtype)

    for scr, v in zip(decay_scr, nxt):
        scr[...] = v


def _ssd_core(xbc, dt_raw, batch, seq, conv_w, conv_b, dt_bias, a_log, d_skip):
    t, conv_dim = xbc.shape
    n_heads = dt_bias.shape[0]
    d_inner = n_heads * SSD_HEAD_DIM
    L = SSD_CHUNK
    nc = seq // L
    pad = lambda v: jnp.pad(v.astype(F32), (0, LANES - n_heads))
    expand = (jnp.arange(LANES)[:, None] == (jnp.arange(d_inner)[None, :] // SSD_HEAD_DIM)).astype(BF16)
    dsk_x = jnp.repeat(d_skip.astype(F32), SSD_HEAD_DIM).reshape(1, d_inner)
    when = _perm_time(jnp.arange(L))
    cum = (when[:, None] <= when[None, :]).astype(BF16)
    return pl.pallas_call(
        functools.partial(_ssd_kernel, d_inner=d_inner),
        grid=(batch, nc),
        in_specs=[pl.BlockSpec((L, conv_dim), lambda b, c: (b * nc + c, 0)),
                  pl.BlockSpec((L, LANES), lambda b, c: (b * nc + c, 0)),
                  pl.BlockSpec((L, LANES), lambda b, c: (b * nc + jnp.minimum(c + 1, nc - 1), 0)),
                  _resident((L, L)),
                  _resident((CONV_TAPS, conv_dim)), _resident((1, conv_dim)),
                  _resident((1, LANES)), _resident((LANES, 1)), _resident((1, d_inner)),
                  _resident((LANES, d_inner))],
        out_specs=pl.BlockSpec((L, d_inner), lambda b, c: (b * nc + c, 0)),
        out_shape=jax.ShapeDtypeStruct((t, d_inner), BF16),
        scratch_shapes=[pltpu.VMEM((SSD_N_GROUPS, SSD_D_STATE, d_inner // SSD_N_GROUPS), F32),
                        pltpu.VMEM((HALO_ROWS, conv_dim), F32),
                        pltpu.VMEM((L, conv_dim), F32),
                        pltpu.VMEM((L, LANES), F32), pltpu.VMEM((n_heads, L), F32), pltpu.VMEM((n_heads, L), F32)]
        + [pltpu.VMEM((L, LANES), BF16)] * 4,
        compiler_params=_cparams("parallel", "arbitrary"),
        name="ssd_core",
    )(xbc, dt_raw, dt_raw, cum, conv_w, conv_b.reshape(1, conv_dim), pad(dt_bias).reshape(1, LANES),
      pad(a_log).reshape(LANES, 1), dsk_x, expand)


def _lru_kernel(xc_ref, wga_ref, bga_ref, wgx_ref, bgx_ref, lam_ref, y_ref, h_scr, a_scr, u_scr):
    tc, width = y_ref.shape
    blk = width // LRU_N_BLOCKS

    @pl.when(pl.program_id(1) == 0)
    def _():
        h_scr[...] = jnp.zeros_like(h_scr)

    neg_sp = _softplus(-lam_ref[...])
    for k in range(LRU_N_BLOCKS):
        sl = slice(k * blk, (k + 1) * blk)
        xb16 = xc_ref[:, sl]
        r = _sigmoid(_dot(xb16, wga_ref[k]) + bga_ref[:, sl])
        i = _sigmoid(_dot(xb16, wgx_ref[k]) + bgx_ref[:, sl])
        log_a = -LRU_C * r * neg_sp[:, sl]
        a_scr[:, sl] = jnp.exp(log_a)
        u_scr[:, sl] = jnp.sqrt(1.0 - jnp.exp(2.0 * log_a)) * (i * xb16.astype(F32))

    row = lax.broadcasted_iota(jnp.int32, (SUBLANES, width), 0)
    h_in = h_scr[...]
    for c0 in range(0, tc, PERM_CHUNK):
        vreg = lambda i: pl.ds(c0 + i * SUBLANES, SUBLANES)
        for i in range(1, PERM_SEG):
            a_i = a_scr[vreg(i), :]
            u_scr[vreg(i), :] = a_i * u_scr[vreg(i - 1), :] + u_scr[vreg(i), :]
            a_scr[vreg(i), :] = a_i * a_scr[vreg(i - 1), :]
        a = a_scr[vreg(PERM_SEG - 1), :]
        u = u_scr[vreg(PERM_SEG - 1), :]
        for k in (1, 2, 4):
            keep = row >= k
            a_sh = jnp.where(keep, pltpu.roll(a, k, 0), 1.0)
            u_sh = jnp.where(keep, pltpu.roll(u, k, 0), 0.0)
            u = a * u_sh + u
            a = a * a_sh
        carry = jnp.where(row == 0, h_in, pltpu.roll(a, 1, 0) * h_in + pltpu.roll(u, 1, 0))
        for i in range(PERM_SEG):
            u_scr[vreg(i), :] = u_scr[vreg(i), :] + a_scr[vreg(i), :] * carry
        last = u_scr[vreg(PERM_SEG - 1), :]
        h_in = jnp.broadcast_to(last[SUBLANES - 1:SUBLANES, :], last.shape)
    h_scr[...] = h_in
    y_ref[...] = u_scr[...].astype(y_ref.dtype)


def _lru_core(xc, batch, seq, w_ga, b_ga, w_gx, b_gx, lam):
    t, width = xc.shape
    tc = LRU_CHUNK
    nc = seq // tc
    vec = lambda v: v.astype(F32).reshape(1, width)
    return pl.pallas_call(
        _lru_kernel,
        grid=(batch, nc),
        in_specs=[pl.BlockSpec((tc, width), lambda b, c: (b * nc + c, 0)),
                  _resident(w_ga.shape), _resident((1, width)), _resident(w_gx.shape),
                  _resident((1, width)), _resident((1, width))],
        out_specs=pl.BlockSpec((tc, width), lambda b, c: (b * nc + c, 0)),
        out_shape=jax.ShapeDtypeStruct((t, width), BF16),
        scratch_shapes=[pltpu.VMEM((SUBLANES, width), F32),
                        pltpu.VMEM((tc, width), F32), pltpu.VMEM((tc, width), F32)],
        compiler_params=_cparams("parallel", "arbitrary"),
        name="lru_core",
    )(xc, w_ga.astype(BF16), vec(b_ga), w_gx.astype(BF16), vec(b_gx), vec(lam))


def _attn_kernel(q_ref, k_ref, v_ref, o_ref, lse_ref, kp_scr, vp_scr, *tile_scr, dilation, span):
    L = span
    width = q_ref.shape[2]
    qblocks = q_ref.shape[1] // L
    n = pl.program_id(1)

    @pl.when(n == 0)
    def _():
        kp_scr[...] = jnp.zeros_like(kp_scr)
        vp_scr[...] = jnp.zeros_like(vp_scr)

    qi = lax.broadcasted_iota(jnp.int32, (L, 2 * L), 0)
    ki = lax.broadcasted_iota(jnp.int32, (L, 2 * L), 1)
    dist = qi + L - ki
    in_band = (dist >= 0) & (dist <= L)
    lane_q = lax.broadcasted_iota(jnp.int32, (L, LANES), 1)
    lane_v = lax.broadcasted_iota(jnp.int32, (2 * L, LANES), 1)
    zero = jnp.zeros((), BF16)
    heads_per_pair = LANES // ATTN_HEAD_DIM

    def block(r, u):
        valid = in_band & (ki + (n * qblocks + u - 1) * L >= 0)
        lse_tile = jnp.zeros((L, LANES), F32)
        o_parts = []
        for j in range(width // LANES):
            sl = slice(j * LANES, (j + 1) * LANES)
            qp = q_ref[r, u * L:(u + 1) * L, sl]
            if u == 0:
                kk = jnp.concatenate([kp_scr[r, :, sl], k_ref[r, 0:L, sl]], axis=0)
                vv = jnp.concatenate([vp_scr[r, :, sl], v_ref[r, 0:L, sl]], axis=0)
            else:
                kk = k_ref[r, (u - 1) * L:(u + 1) * L, sl]
                vv = v_ref[r, (u - 1) * L:(u + 1) * L, sl]
            acc = jnp.zeros((L, LANES), F32)
            for hh in range(heads_per_pair):
                in_head_q = lane_q < ATTN_HEAD_DIM if hh == 0 else lane_q >= ATTN_HEAD_DIM
                in_head_v = lane_v < ATTN_HEAD_DIM if hh == 0 else lane_v >= ATTN_HEAD_DIM
                s = _dot_nt(jnp.where(in_head_q, qp, zero), kk)
                s = jnp.where(valid, s, MASK_VALUE)
                m = jnp.max(s, axis=-1, keepdims=True)
                p = jnp.exp(s - m)
                l = jnp.sum(p, axis=-1, keepdims=True)
                o = _dot(p.astype(BF16), jnp.where(in_head_v, vv, zero))
                acc = acc + o / l
                lse_tile = jnp.where(lane_q == j * heads_per_pair + hh, m + jnp.log(l), lse_tile)
            o_parts.append(acc)
        return o_parts, lse_tile

    def residue(r, carry):
        for u in range(qblocks):
            o_parts, lse_tile = block(r, u)
            if dilation == 1:
                o_ref[u * L:(u + 1) * L, :] = jnp.concatenate(o_parts, axis=1).astype(o_ref.dtype)
                lse_ref[u * L:(u + 1) * L, :] = lse_tile
            else:
                o_scr, lse_scr = tile_scr
                rows = pl.ds(u * L * dilation + r, L, stride=dilation)
                for j, part in enumerate(o_parts):
                    o_scr[j, rows, :] = part
                lse_scr[rows, :] = lse_tile
        kp_scr[r] = k_ref[r, (qblocks - 1) * L:qblocks * L, :]
        vp_scr[r] = v_ref[r, (qblocks - 1) * L:qblocks * L, :]
        return carry

    if dilation == 1:
        residue(0, 0)
    else:
        lax.fori_loop(0, dilation, residue, 0, unroll=2)
        o_scr, lse_scr = tile_scr
        for j in range(width // LANES):
            o_ref[:, j * LANES:(j + 1) * LANES] = o_scr[j].astype(o_ref.dtype)
        lse_ref[...] = lse_scr[...]


def _attn_group(qkv_g, batch, seq, dilation, span):
    width = ATTN_HEADS * ATTN_HEAD_DIM
    m = seq // dilation
    qblocks = 2 if dilation == 1 else 1
    nsteps = m // (span * qblocks)
    t = batch * seq
    rows = span * qblocks * dilation
    spec = lambda which: pl.BlockSpec((None, dilation, span * qblocks, width), lambda b, n: (b, 0, n, which))
    scratch = [pltpu.VMEM((dilation, span, width), BF16), pltpu.VMEM((dilation, span, width), BF16)]
    if dilation > 1:
        scratch += [pltpu.VMEM((width // LANES, rows, LANES), F32), pltpu.VMEM((rows, LANES), F32)]
    return pl.pallas_call(
        functools.partial(_attn_kernel, dilation=dilation, span=span),
        grid=(batch, nsteps),
        in_specs=[spec(0), spec(1), spec(2)],
        out_specs=[pl.BlockSpec((rows, width), lambda b, n: (b * nsteps + n, 0)),
                   pl.BlockSpec((rows, LANES), lambda b, n: (b * nsteps + n, 0))],
        out_shape=[jax.ShapeDtypeStruct((t, width), BF16),
                   jax.ShapeDtypeStruct((t, LANES), F32)],
        scratch_shapes=scratch,
        compiler_params=_cparams("parallel", "arbitrary"),
        name=f"attn_d{dilation}",
    )(qkv_g, qkv_g, qkv_g)


def _attn_out_kernel(o0_ref, o1_ref, o2_ref, l0_ref, l1_ref, l2_ref, w_ref, x_ref, gate_ref, gp_ref,
                     out_ref):
    tm, width = o0_ref.shape
    lane = lax.broadcasted_iota(jnp.int32, (tm, LANES), 1)
    first_head = lane < ATTN_HEAD_DIM
    lses = (l0_ref[...], l1_ref[...], l2_ref[...])
    outs = (o0_ref, o1_ref, o2_ref)
    parts = []
    for j in range(width // LANES):
        sl = slice(j * LANES, (j + 1) * LANES)
        ls = [jnp.where(first_head, l[:, 2 * j:2 * j + 1], l[:, 2 * j + 1:2 * j + 2]) for l in lses]
        mx = jnp.maximum(jnp.maximum(ls[0], ls[1]), ls[2])
        es = [jnp.exp(l - mx) for l in ls]
        num = sum(e * o[:, sl].astype(F32) for e, o in zip(es, outs))
        parts.append(num / (es[0] + es[1] + es[2]))
    o = jnp.concatenate(parts, axis=1).astype(BF16)
    out_ref[...] = _residual(x_ref[...], _dot(o, w_ref[...]), gate_ref[...], gp_ref[...])


def _attn_out(outs, lses, w, x, seq, modl, which, g_post):
    t, d = x.shape
    width = w.shape[0]
    tm = ROW_TILE
    per = seq // tm
    row = lambda n: pl.BlockSpec((tm, n), lambda i: (i, 0))
    return pl.pallas_call(
        _attn_out_kernel,
        grid=(t // tm,),
        in_specs=[row(width)] * 3 + [row(LANES)] * 3 + [
            _resident((width, d)),
            row(d),
            pl.BlockSpec((None, 1, d), lambda i: ((i // per) * N_MOD + which, 0, 0)),
            pl.BlockSpec((1, d), lambda i: (0, 0))],
        out_specs=row(d),
        out_shape=jax.ShapeDtypeStruct((t, d), F32),
        compiler_params=_cparams("parallel"),
        name="attn_out",
    )(*outs, *lses, w, x, modl, g_post.reshape(1, d))


def _rope_tables(positions):
    half = ROPE_DIM // 2
    inv_freq = ROPE_THETA ** (-jnp.arange(0, ROPE_DIM, 2, dtype=F32) / ROPE_DIM)
    ang = positions.astype(F32).reshape(-1, 1) * inv_freq
    cos, sin = jnp.cos(ang), jnp.sin(ang)
    t = cos.shape[0]
    rest = ATTN_HEAD_DIM - ROPE_DIM
    zeros = jnp.zeros((t, half), F32)
    c = jnp.concatenate([cos, cos, jnp.ones((t, rest), F32)], axis=1)
    s1 = jnp.concatenate([-sin, zeros, jnp.zeros((t, rest), F32)], axis=1)
    s2 = jnp.concatenate([zeros, sin, jnp.zeros((t, rest), F32)], axis=1)
    reps = LANES // ATTN_HEAD_DIM
    return tuple(jnp.tile(v, (1, reps)) for v in (c, s1, s2))


def _attention_layer(xf, batch, seq, g_pre, g_post, modl, rope, w_qkv, w_out):
    dilations = [d for _, d in ATTN_CONFIGS]
    qkvs = _inproj_qkv(xf, seq, g_pre, modl, 0, w_qkv, rope, dilations, ATTN_HEAD_DIM ** -0.5)
    outs, lses = [], []
    for qkv_g, (window, dilation) in zip(qkvs, ATTN_CONFIGS):
        qkv_g = qkv_g.reshape(batch, dilation, seq // dilation, qkv_g.shape[-1])
        o, lse = _attn_group(qkv_g, batch, seq, dilation, window // dilation)
        outs.append(o)
        lses.append(lse)
    return _attn_out(outs, lses, w_out, xf, seq, modl, 2, g_post)


def kernel(x, c, positions, ada_w, ada_b, norm_mix_pre, norm_mix_post, norm_mlp_pre, norm_mlp_post, mlp_w1, mlp_w2, ssd_w_in, ssd_conv_w, ssd_conv_b, ssd_dt_bias, ssd_a_log, ssd_d, ssd_norm, ssd_w_out, lru_w_in, lru_conv_w, lru_conv_b, lru_w_gate_a, lru_b_gate_a, lru_w_gate_x, lru_b_gate_x, lru_lambda, lru_w_out, attn_w_qkv, attn_w_out):
    batch, seq, d = x.shape
    depth = ada_w.shape[0]
    xf = x.reshape(batch * seq, d)
    mod = _modulation(c, ada_w, ada_b)
    rope = _rope_tables(positions)
    w1_all, w2_all = mlp_w1.astype(BF16), mlp_w2.astype(BF16)
    for layer in range(depth):
        modl = mod[layer].reshape(batch * N_MOD, 1, d)
        kind, occ = layer % N_MIXERS, layer // N_MIXERS
        if kind == 0:
            w_in = ssd_w_in[occ]
            n_heads = ssd_dt_bias.shape[1]
            d_inner = n_heads * SSD_HEAD_DIM
            n_main = w_in.shape[1] - n_heads
            w_dt = jnp.pad(w_in[:, n_main:], ((0, 0), (0, LANES - n_heads))).astype(BF16)
            z, xbc, dt_raw = _inproj(xf, seq, norm_mix_pre[layer], modl, 0, w_in[:, :n_main].astype(BF16),
                                     d_inner, w_dt=w_dt)
            y = _ssd_core(xbc, dt_raw, batch, seq, ssd_conv_w[occ], ssd_conv_b[occ], ssd_dt_bias[occ],
                          ssd_a_log[occ], ssd_d[occ])
            xf = _outproj(y, z, ssd_w_out[occ].astype(BF16), xf, seq, modl, 2, norm_mix_post[layer],
                          norm_g=ssd_norm[occ], n_groups=SSD_N_GROUPS)
        elif kind == 1:
            width = lru_conv_w.shape[-1]
            gate, xc = _inproj(xf, seq, norm_mix_pre[layer], modl, 0, lru_w_in[occ].astype(BF16), width,
                               conv_w=lru_conv_w[occ], conv_b=lru_conv_b[occ])
            hs = _lru_core(xc, batch, seq, lru_w_gate_a[occ], lru_b_gate_a[occ], lru_w_gate_x[occ],
                           lru_b_gate_x[occ], lru_lambda[occ])
            xf = _outproj(hs, gate, lru_w_out[occ].astype(BF16), xf, seq, modl, 2, norm_mix_post[layer])
        else:
            xf = _attention_layer(xf, batch, seq, norm_mix_pre[layer], norm_mix_post[layer], modl, rope,
                                  attn_w_qkv[occ].astype(BF16), attn_w_out[occ].astype(BF16))
        xf = _mlp(xf, seq, modl, norm_mlp_pre[layer], norm_mlp_post[layer], w1_all, w2_all, layer)
    return xf.reshape(batch, seq, d)
```

```python
import functools

import jax
import jax.numpy as jnp
from jax import lax
from jax.experimental import pallas as pl
from jax.experimental.pallas import tpu as pltpu

F32 = jnp.float32
BF16 = jnp.bfloat16

NORM_EPS = 1e-6
N_MIXERS = 3
N_MOD = 6

SSD_HEAD_DIM = 64
SSD_N_GROUPS = 8
SSD_D_STATE = 128
SSD_CHUNK = 128

LRU_N_BLOCKS = 4
LRU_C = 8.0
LRU_CHUNK = 256

ATTN_HEAD_DIM = 64
ATTN_HEADS = 8
ATTN_CONFIGS = ((128, 1), (512, 4), (2048, 16))
ROPE_THETA = 500000.0
ROPE_DIM = ATTN_HEAD_DIM // 4
MASK_VALUE = -1e30

LANES = 128
SUBLANES = 8
VMEM_LIMIT = 56 * 1024 * 1024

ROW_TILE = 512
COL_TILE = 512


def _cparams(*sem):
    return pltpu.CompilerParams(dimension_semantics=sem, vmem_limit_bytes=VMEM_LIMIT)


def _dot(a, b):
    return jnp.dot(a, b, preferred_element_type=F32)


def _dot_nt(a, b):
    return lax.dot_general(a, b, (((1,), (1,)), ((), ())), preferred_element_type=F32)


def _split3(v):
    hi = v.astype(BF16)
    r1 = v - hi.astype(F32)
    mid = r1.astype(BF16)
    lo = (r1 - mid.astype(F32)).astype(BF16)
    return hi, mid, lo


def _rms(y):
    return y * lax.rsqrt(jnp.mean(y * y, axis=-1, keepdims=True) + NORM_EPS)


def _norm_mod(x, g, sc, sh):
    return _rms(x) * g * (1.0 + sc) + sh


def _residual(x, y, gate, g_post):
    return x + (1.0 + gate) * (_rms(y) * g_post)


def _softplus(x):
    return jnp.maximum(x, 0.0) + jnp.log1p(jnp.exp(-jnp.abs(x)))


def _sigmoid(x):
    return 0.5 + 0.5 * jnp.tanh(0.5 * x)


def _silu_of_twice(h):
    return h + h * jnp.tanh(h)


def _silu(x):
    return _silu_of_twice(0.5 * x)


CONV_TAPS = 4


PERM_CHUNK = SSD_CHUNK
PERM_SEG = PERM_CHUNK // SUBLANES
HALO_ROWS = (CONV_TAPS - 1) * SUBLANES


def _perm_time(r):
    return (r % SUBLANES) * PERM_SEG + r // SUBLANES


def _causal_conv(x, halo, w, b):
    n = PERM_CHUNK
    row8 = lax.broadcasted_iota(jnp.int32, (SUBLANES, x.shape[1]), 0)
    wrapped = []
    for k in range(CONV_TAPS - 1):
        cur = x[n - HALO_ROWS + k * SUBLANES:n - HALO_ROWS + (k + 1) * SUBLANES]
        prev = halo[k * SUBLANES:(k + 1) * SUBLANES]
        wrapped.append(jnp.where(row8 == 0, pltpu.roll(prev, 1, 0), pltpu.roll(cur, 1, 0)))
    acc = x * w[CONV_TAPS - 1:CONV_TAPS, :] + b
    for j in range(1, CONV_TAPS):
        xj = jnp.concatenate(wrapped[CONV_TAPS - 1 - j:] + [x[0:n - j * SUBLANES]], axis=0)
        acc = acc + xj * w[CONV_TAPS - 1 - j:CONV_TAPS - j, :]
    return acc


def _permute_rows(v, scr, to_time_order):
    tm = v.shape[0]
    nb = v.shape[1] // LANES
    for j in range(nb):
        scr[j] = v[:, j * LANES:(j + 1) * LANES]
    cols = []
    for j in range(nb):
        rows = []
        for c0 in range(0, tm, PERM_CHUNK):
            for a in range(PERM_CHUNK // SUBLANES):
                if to_time_order:
                    s, i0 = divmod(a * SUBLANES, PERM_SEG)
                    rows.append(scr[j, pl.ds(c0 + i0 * SUBLANES + s, SUBLANES, stride=SUBLANES), :])
                else:
                    rows.append(scr[j, pl.ds(c0 + a, SUBLANES, stride=PERM_SEG), :])
        cols.append(jnp.concatenate(rows, axis=0))
    return jnp.concatenate(cols, axis=1)


def _resident(shape):
    return pl.BlockSpec(shape, lambda *_: (0,) * len(shape), pipeline_mode=pl.Buffered(1))


def _mod_kernel(c_ref, w_ref, b_ref, o_ref):
    c = c_ref[...]
    a_hi, a_mid, _ = _split3(_silu(c))
    w = w_ref[...]
    w_hi = w.astype(BF16)
    w_lo = (w - w_hi.astype(F32)).astype(BF16)
    rows = c.shape[0]
    both = _dot(jnp.concatenate([a_hi, a_mid], axis=0), w_hi)
    o_ref[...] = both[0:rows] + both[rows:2 * rows] + _dot(a_hi, w_lo) + b_ref[...]


def _modulation(c, ada_w, ada_b):
    depth, d, e = ada_w.shape
    b = c.shape[0]
    tn = 1024
    return pl.pallas_call(
        _mod_kernel,
        grid=(depth, e // tn),
        in_specs=[pl.BlockSpec((b, d), lambda l, j: (0, 0)),
                  pl.BlockSpec((None, d, tn), lambda l, j: (l, 0, j)),
                  pl.BlockSpec((None, 1, tn), lambda l, j: (l, 0, j))],
        out_specs=pl.BlockSpec((None, b, tn), lambda l, j: (l, 0, j)),
        out_shape=jax.ShapeDtypeStruct((depth, b, e), F32),
        compiler_params=_cparams("parallel", "parallel"),
        name="adaln_mod",
    )(c, ada_w, ada_b.reshape(depth, 1, e))


def _rope(y, rc, rs1, rs2):
    reps = y.shape[1] // LANES
    tile = lambda t: jnp.concatenate([t] * reps, axis=1)
    n = y.shape[1]
    half = ROPE_DIM // 2
    return y * tile(rc) + pltpu.roll(y, n - half, 1) * tile(rs1) + pltpu.roll(y, half, 1) * tile(rs2)


def _inproj_kernel(*refs, n_first, per, has_conv, has_dt):
    x_ref, g_ref, sc_ref, sh_ref, w_ref = refs[:5]
    pos = 5
    if has_conv:
        cw_ref, cb_ref = refs[pos:pos + 2]
        pos += 2
    if has_dt:
        wdt_ref = refs[pos]
        pos += 1
    first_ref, second_ref = refs[pos:pos + 2]
    pos += 2
    if has_dt:
        odt_ref = refs[pos]
        pos += 1
    h_scr, perm_scr = refs[pos:pos + 2]
    tm = x_ref.shape[0]
    tn = COL_TILE

    if has_conv:
        halo_scr = refs[pos + 2]

        @pl.when(pl.program_id(0) % per == 0)
        def _():
            halo_scr[...] = jnp.zeros_like(halo_scr)

    h = _norm_mod(x_ref[...], g_ref[...], sc_ref[...], sh_ref[...])
    h_scr[...] = _permute_rows(h, perm_scr, to_time_order=False).astype(BF16)
    if has_dt:
        odt_ref[...] = _dot(h_scr[...], wdt_ref[...])
    for c0 in range(0, w_ref.shape[1], tn):
        acc = _dot(h_scr[...], w_ref[:, c0:c0 + tn])
        cc = slice(c0 - n_first, c0 - n_first + tn)
        if c0 < n_first:
            first_ref[:, c0:c0 + tn] = acc.astype(BF16)
        elif has_conv:
            halo = halo_scr[:, cc]
            for r0 in range(0, tm, PERM_CHUNK):
                x_c = acc[r0:r0 + PERM_CHUNK]
                second_ref[r0:r0 + PERM_CHUNK, cc] = _causal_conv(x_c, halo, cw_ref[:, cc], cb_ref[:, cc]).astype(BF16)
                halo = x_c[PERM_CHUNK - HALO_ROWS:]
            halo_scr[:, cc] = halo
        else:
            second_ref[:, cc] = acc.astype(BF16)


def _inproj(x, seq, g, modl, which, w, n_first, conv_w=None, conv_b=None, w_dt=None):
    t, d = x.shape
    n = w.shape[1]
    n_second = n - n_first
    tm = ROW_TILE
    per = seq // tm
    has_conv = conv_w is not None
    in_specs = [pl.BlockSpec((tm, d), lambda i: (i, 0)),
                pl.BlockSpec((1, d), lambda i: (0, 0)),
                pl.BlockSpec((None, 1, d), lambda i: ((i // per) * N_MOD + which + 1, 0, 0)),
                pl.BlockSpec((None, 1, d), lambda i: ((i // per) * N_MOD + which, 0, 0)),
                _resident((d, n))]
    args = [x, g.reshape(1, d), modl, modl, w]
    scratch = [pltpu.VMEM((tm, d), BF16), pltpu.VMEM((d // LANES, tm, LANES), F32)]
    if has_conv:
        in_specs += [_resident((CONV_TAPS, n_second)), _resident((1, n_second))]
        args += [conv_w, conv_b.reshape(1, n_second)]
        scratch.append(pltpu.VMEM((HALO_ROWS, n_second), F32))
    out_specs = [pl.BlockSpec((tm, n_first), lambda i: (i, 0)), pl.BlockSpec((tm, n_second), lambda i: (i, 0))]
    out_shape = [jax.ShapeDtypeStruct((t, n_first), BF16), jax.ShapeDtypeStruct((t, n_second), BF16)]
    if w_dt is not None:
        in_specs.append(_resident((d, LANES)))
        args.append(w_dt)
        out_specs.append(pl.BlockSpec((tm, LANES), lambda i: (i, 0)))
        out_shape.append(jax.ShapeDtypeStruct((t, LANES), F32))
    return pl.pallas_call(
        functools.partial(_inproj_kernel, n_first=n_first, per=per, has_conv=has_conv, has_dt=w_dt is not None),
        grid=(t // tm,),
        in_specs=in_specs,
        out_specs=out_specs,
        out_shape=out_shape,
        scratch_shapes=scratch,
        compiler_params=_cparams("arbitrary" if has_conv else "parallel"),
        name="inproj",
    )(*args)


def _inproj_qkv_kernel(x_ref, g_ref, sc_ref, sh_ref, w_ref, rc_ref, rs1_ref, rs2_ref, *refs, dilations,
                       q_scale):
    out_refs = refs[:len(dilations)]
    h_scr, tile_scr = refs[len(dilations):]
    tm = x_ref.shape[0]
    tn = COL_TILE

    h_scr[...] = _norm_mod(x_ref[...], g_ref[...], sc_ref[...], sh_ref[...]).astype(BF16)
    for c0 in range(0, w_ref.shape[1], tn):
        acc = _dot(h_scr[...], w_ref[:, c0:c0 + tn])
        grp, kind = divmod(c0 // tn, 3)
        if kind < 2:
            acc = _rope(acc, rc_ref[...], rs1_ref[...], rs2_ref[...])
            if kind == 0:
                acc = acc * q_scale
        o_ref, d = out_refs[grp], dilations[grp]
        if d == 1:
            o_ref[0, :, kind * tn:(kind + 1) * tn] = acc.astype(BF16)
        else:
            for j in range(tn // LANES):
                tile_scr[j] = acc[:, j * LANES:(j + 1) * LANES]
            for r in range(d):
                for j in range(tn // LANES):
                    o_ref[r, :, kind * tn + j * LANES:kind * tn + (j + 1) * LANES] = (
                        tile_scr[j, pl.ds(r, tm // d, stride=d), :].astype(BF16))


def _inproj_qkv(x, seq, g, modl, which, w, rope, dilations, q_scale):
    t, d = x.shape
    n = w.shape[1]
    tm = ROW_TILE
    per = seq // tm
    batch = t // seq
    gw = n // len(dilations)
    in_specs = [pl.BlockSpec((tm, d), lambda i: (i, 0)),
                pl.BlockSpec((1, d), lambda i: (0, 0)),
                pl.BlockSpec((None, 1, d), lambda i: ((i // per) * N_MOD + which + 1, 0, 0)),
                pl.BlockSpec((None, 1, d), lambda i: ((i // per) * N_MOD + which, 0, 0)),
                _resident((d, n))] + [pl.BlockSpec((tm, LANES), lambda i: (i, 0))] * 3
    return pl.pallas_call(
        functools.partial(_inproj_qkv_kernel, dilations=tuple(dilations), q_scale=q_scale),
        grid=(t // tm,),
        in_specs=in_specs,
        out_specs=[pl.BlockSpec((None, dl, None, tm // dl, gw), lambda i: (i // per, 0, i % per, 0, 0))
                   for dl in dilations],
        out_shape=[jax.ShapeDtypeStruct((batch, dl, per, tm // dl, gw), BF16) for dl in dilations],
        scratch_shapes=[pltpu.VMEM((tm, d), BF16), pltpu.VMEM((COL_TILE // LANES, tm, LANES), F32)],
        compiler_params=_cparams("parallel"),
        name="inproj_qkv",
    )(x, g.reshape(1, d), modl, modl, w, *rope)


def _outproj_kernel(y_ref, z_ref, *refs, n_groups):
    if n_groups:
        ng_ref, *refs = refs
    w_ref, x_ref, gate_ref, gp_ref, o_ref, perm_scr = refs
    y = y_ref[...].astype(F32)
    z = z_ref[...].astype(F32)
    if n_groups:
        y = y * _silu_of_twice(z)
        gw = y.shape[1] // n_groups
        y = jnp.concatenate([_rms(y[:, g * gw:(g + 1) * gw]) for g in range(n_groups)], axis=1) * ng_ref[...]
    else:
        y = y * jax.nn.gelu(z, approximate=True)
    out = _permute_rows(_dot(y.astype(BF16), w_ref[...]), perm_scr, to_time_order=True)
    o_ref[...] = _residual(x_ref[...], out, gate_ref[...], gp_ref[...])


def _outproj(y, z, w, x, seq, modl, which, g_post, norm_g=None, n_groups=0):
    t, kdim = y.shape
    d = w.shape[1]
    tm = ROW_TILE
    per = seq // tm
    row = pl.BlockSpec((tm, kdim), lambda i: (i, 0))
    in_specs = [row, row]
    args = [y, z]
    if n_groups:
        in_specs.append(pl.BlockSpec((1, kdim), lambda i: (0, 0)))
        args.append(norm_g.reshape(1, kdim))
    in_specs += [_resident((kdim, d)),
                 pl.BlockSpec((tm, d), lambda i: (i, 0)),
                 pl.BlockSpec((None, 1, d), lambda i: ((i // per) * N_MOD + which, 0, 0)),
                 pl.BlockSpec((1, d), lambda i: (0, 0))]
    return pl.pallas_call(
        functools.partial(_outproj_kernel, n_groups=n_groups),
        grid=(t // tm,),
        in_specs=in_specs,
        out_specs=pl.BlockSpec((tm, d), lambda i: (i, 0)),
        out_shape=jax.ShapeDtypeStruct((t, d), F32),
        scratch_shapes=[pltpu.VMEM((d // LANES, tm, LANES), F32)],
        compiler_params=_cparams("parallel"),
        name="outproj",
    )(*args, w, x, modl, g_post.reshape(1, d))


def _mlp_kernel(x_ref, g_ref, sc_ref, sh_ref, w1_ref, w2_ref, gate_ref, gp_ref, o_ref, h_scr):
    h_scr[...] = _norm_mod(x_ref[...], g_ref[...], sc_ref[...], sh_ref[...]).astype(BF16)
    tf = COL_TILE
    acc = None
    for f0 in range(0, w1_ref.shape[1], tf):
        a = jnp.maximum(_dot(h_scr[...], w1_ref[:, f0:f0 + tf]), 0.0)
        part = _dot((a * a).astype(BF16), w2_ref[f0:f0 + tf, :])
        acc = part if acc is None else acc + part
    o_ref[...] = _residual(x_ref[...], acc, gate_ref[...], gp_ref[...])


def _mlp(x, seq, modl, g_pre, g_post, w1, w2, layer):
    t, d = x.shape
    dff = w1.shape[2]
    tm = ROW_TILE
    per = seq // tm
    mod_spec = lambda which: pl.BlockSpec((None, 1, d), lambda i: ((i // per) * N_MOD + which, 0, 0))
    slab = lambda r, c: pl.BlockSpec((None, r, c), lambda i: (layer, 0, 0), pipeline_mode=pl.Buffered(1))
    return pl.pallas_call(
        _mlp_kernel,
        grid=(t // tm,),
        in_specs=[pl.BlockSpec((tm, d), lambda i: (i, 0)),
                  pl.BlockSpec((1, d), lambda i: (0, 0)),
                  mod_spec(4), mod_spec(3),
                  slab(d, dff), slab(dff, d),
                  mod_spec(5),
                  pl.BlockSpec((1, d), lambda i: (0, 0))],
        out_specs=pl.BlockSpec((tm, d), lambda i: (i, 0)),
        out_shape=jax.ShapeDtypeStruct((t, d), F32),
        scratch_shapes=[pltpu.VMEM((tm, d), BF16)],
        compiler_params=_cparams("parallel"),
        name="mlp",
    )(x, g_pre.reshape(1, d), modl, modl, w1, w2, modl, g_post.reshape(1, d))


def _ssd_decays(dt_raw, dt_bias, a_log_col, n_heads, cum):
    L = dt_raw.shape[0]
    dt_t = _softplus((dt_raw + dt_bias).T[0:n_heads])
    a_col = -jnp.exp(a_log_col[0:n_heads, :])
    a_cs_t = sum(_dot(term, cum) for term in _split3(dt_t * a_col))
    e_cs_t = jnp.exp(a_cs_t)
    dte_t = jnp.exp(a_cs_t[:, L - 1:L] - a_cs_t) * dt_t
    to_cols = lambda v: jnp.concatenate([v, jnp.zeros((LANES - n_heads, L), F32)], axis=0).T
    e_hi, e_mid, _ = _split3(to_cols(e_cs_t))
    d_hi, d_mid, _ = _split3(to_cols(dte_t))
    return to_cols(a_cs_t), a_cs_t - jnp.log(dt_t), e_hi, e_mid, d_hi, d_mid


def _ssd_kernel(xbc_ref, dt_ref, dtn_ref, cum_ref, cw_ref, cb_ref, dtb_ref, alog_ref, dsk_ref, exp_ref, y_ref,
                state_scr, tail_scr, act_scr, *decay_scr, d_inner):
    L, N, G = SSD_CHUNK, SSD_D_STATE, SSD_N_GROUPS
    H = d_inner // SSD_HEAD_DIM
    gw = d_inner // G
    conv_dim = d_inner + 2 * G * N
    c = pl.program_id(1)

    @pl.when(c == 0)
    def _():
        state_scr[...] = jnp.zeros_like(state_scr)
        tail_scr[...] = jnp.zeros_like(tail_scr)
        for scr, v in zip(decay_scr, _ssd_decays(dt_ref[...], dtb_ref[...], alog_ref[...], H, cum_ref[...])):
            scr[...] = v

    a_cs, src_t, e_hi, e_mid, d_hi, d_mid = [scr[...] for scr in decay_scr]
    nxt = _ssd_decays(dtn_ref[...], dtb_ref[...], alog_ref[...], H, cum_ref[...])

    strip = 512
    for s0 in range(0, conv_dim, strip):
        cols = slice(s0, s0 + strip)
        xin = xbc_ref[:, cols].astype(F32)
        act_scr[:, cols] = _silu_of_twice(_causal_conv(xin, tail_scr[:, cols], cw_ref[:, cols], cb_ref[:, cols]))
        tail_scr[:, cols] = xin[L - HALO_ROWS:L]

    row = lax.broadcasted_iota(jnp.int32, (L, L), 0)
    col = lax.broadcasted_iota(jnp.int32, (L, L), 1)
    tri = _perm_time(row) >= _perm_time(col)
    act = lambda lo, hi: act_scr[:, lo:hi]
    lane = lax.broadcasted_iota(jnp.int32, (L, LANES), 1)
    first_head = lane < SSD_HEAD_DIM
    heads_per_pair = LANES // SSD_HEAD_DIM
    for g in range(G):
        gs = slice(g * gw, (g + 1) * gw)
        expand = exp_ref[:, gs]
        e_cs_x = _dot(e_hi, expand) + _dot(e_mid, expand)
        dte_x = _dot(d_hi, expand) + _dot(d_mid, expand)
        xs_g = act(g * gw, (g + 1) * gw)
        bm_g = act(d_inner + g * N, d_inner + (g + 1) * N)
        cm_g = act(d_inner + (G + g) * N, d_inner + (G + g + 1) * N).astype(BF16)
        cb16 = _dot_nt(cm_g, bm_g.astype(BF16)).astype(BF16)
        st_prev = state_scr[g]
        y_off = _dot(cm_g, st_prev.astype(BF16)) * e_cs_x
        w_g = (xs_g * dte_x).astype(BF16)
        new_state = _dot(bm_g.T.astype(BF16), w_g)
        state_scr[g] = st_prev * e_cs_x[L - 1:L, :] + new_state
        y_parts = []
        for q in range(gw // LANES):
            xs_p = xs_g[:, q * LANES:(q + 1) * LANES]
            ms = []
            for hh in range(heads_per_pair):
                h = (g * gw + q * LANES) // SSD_HEAD_DIM + hh
                seg = a_cs[:, h:h + 1] - src_t[h:h + 1, :]
                ms.append(cb16 * jnp.exp(jnp.where(tri, seg, -jnp.inf).astype(BF16)))
            m_cat = jnp.concatenate(ms, axis=1)
            x_cat = jnp.concatenate([jnp.where(first_head, xs_p, 0.0),
                                     jnp.where(first_head, 0.0, xs_p)], axis=0).astype(BF16)
            y_parts.append(_dot(m_cat, x_cat))
        y = jnp.concatenate(y_parts, axis=1) + y_off + xs_g * dsk_ref[:, gs]
        y_ref[:, gs] = y.astype(y_ref.dtype)

    for scr, v in zip(decay_scr, nxt):
        scr[...] = v


def _ssd_core(xbc, dt_raw, batch, seq, conv_w, conv_b, dt_bias, a_log, d_skip):
    t, conv_dim = xbc.shape
    n_heads = dt_bias.shape[0]
    d_inner = n_heads * SSD_HEAD_DIM
    L = SSD_CHUNK
    nc = seq // L
    pad = lambda v: jnp.pad(v.astype(F32), (0, LANES - n_heads))
    expand = (jnp.arange(LANES)[:, None] == (jnp.arange(d_inner)[None, :] // SSD_HEAD_DIM)).astype(BF16)
    dsk_x = jnp.repeat(d_skip.astype(F32), SSD_HEAD_DIM).reshape(1, d_inner)
    when = _perm_time(jnp.arange(L))
    cum = (when[:, None] <= when[None, :]).astype(BF16)
    return pl.pallas_call(
        functools.partial(_ssd_kernel, d_inner=d_inner),
        grid=(batch, nc),
        in_specs=[pl.BlockSpec((L, conv_dim), lambda b, c: (b * nc + c, 0)),
                  pl.BlockSpec((L, LANES), lambda b, c: (b * nc + c, 0)),
                  pl.BlockSpec((L, LANES), lambda b, c: (b * nc + jnp.minimum(c + 1, nc - 1), 0)),
                  _resident((L, L)),
                  _resident((CONV_TAPS, conv_dim)), _resident((1, conv_dim)),
                  _resident((1, LANES)), _resident((LANES, 1)), _resident((1, d_inner)),
                  _resident((LANES, d_inner))],
        out_specs=pl.BlockSpec((L, d_inner), lambda b, c: (b * nc + c, 0)),
        out_shape=jax.ShapeDtypeStruct((t, d_inner), BF16),
        scratch_shapes=[pltpu.VMEM((SSD_N_GROUPS, SSD_D_STATE, d_inner // SSD_N_GROUPS), F32),
                        pltpu.VMEM((HALO_ROWS, conv_dim), F32),
                        pltpu.VMEM((L, conv_dim), F32),
                        pltpu.VMEM((L, LANES), F32), pltpu.VMEM((n_heads, L), F32)]
        + [pltpu.VMEM((L, LANES), BF16)] * 4,
        compiler_params=_cparams("parallel", "arbitrary"),
        name="ssd_core",
    )(xbc, dt_raw, dt_raw, cum, 0.5 * conv_w, 0.5 * conv_b.reshape(1, conv_dim), pad(dt_bias).reshape(1, LANES),
      pad(a_log).reshape(LANES, 1), dsk_x, expand)


def _lru_kernel(xc_ref, wga_ref, bga_ref, wgx_ref, bgx_ref, lam_ref, y_ref, h_scr, a_scr, u_scr):
    tc, width = y_ref.shape
    blk = width // LRU_N_BLOCKS

    @pl.when(pl.program_id(1) == 0)
    def _():
        h_scr[...] = jnp.zeros_like(h_scr)

    neg_sp = _softplus(-lam_ref[...])
    for k in range(LRU_N_BLOCKS):
        sl = slice(k * blk, (k + 1) * blk)
        xb16 = xc_ref[:, sl]
        r = _sigmoid(_dot(xb16, wga_ref[k]) + bga_ref[:, sl])
        i = _sigmoid(_dot(xb16, wgx_ref[k]) + bgx_ref[:, sl])
        log_a = -LRU_C * r * neg_sp[:, sl]
        a_scr[:, sl] = jnp.exp(log_a)
        u_scr[:, sl] = jnp.sqrt(1.0 - jnp.exp(2.0 * log_a)) * (i * xb16.astype(F32))

    row = lax.broadcasted_iota(jnp.int32, (SUBLANES, width), 0)
    h_in = h_scr[...]
    for c0 in range(0, tc, PERM_CHUNK):
        vreg = lambda i: pl.ds(c0 + i * SUBLANES, SUBLANES)
        for i in range(1, PERM_SEG):
            a_i = a_scr[vreg(i), :]
            u_scr[vreg(i), :] = a_i * u_scr[vreg(i - 1), :] + u_scr[vreg(i), :]
            a_scr[vreg(i), :] = a_i * a_scr[vreg(i - 1), :]
        a = a_scr[vreg(PERM_SEG - 1), :]
        u = u_scr[vreg(PERM_SEG - 1), :]
        for k in (1, 2, 4):
            keep = row >= k
            a_sh = jnp.where(keep, pltpu.roll(a, k, 0), 1.0)
            u_sh = jnp.where(keep, pltpu.roll(u, k, 0), 0.0)
            u = a * u_sh + u
            a = a * a_sh
        carry = jnp.where(row == 0, h_in, pltpu.roll(a, 1, 0) * h_in + pltpu.roll(u, 1, 0))
        for i in range(PERM_SEG):
            u_scr[vreg(i), :] = u_scr[vreg(i), :] + a_scr[vreg(i), :] * carry
        last = u_scr[vreg(PERM_SEG - 1), :]
        h_in = jnp.broadcast_to(last[SUBLANES - 1:SUBLANES, :], last.shape)
    h_scr[...] = h_in
    y_ref[...] = u_scr[...].astype(y_ref.dtype)


def _lru_core(xc, batch, seq, w_ga, b_ga, w_gx, b_gx, lam):
    t, width = xc.shape
    tc = LRU_CHUNK
    nc = seq // tc
    vec = lambda v: v.astype(F32).reshape(1, width)
    return pl.pallas_call(
        _lru_kernel,
        grid=(batch, nc),
        in_specs=[pl.BlockSpec((tc, width), lambda b, c: (b * nc + c, 0)),
                  _resident(w_ga.shape), _resident((1, width)), _resident(w_gx.shape),
                  _resident((1, width)), _resident((1, width))],
        out_specs=pl.BlockSpec((tc, width), lambda b, c: (b * nc + c, 0)),
        out_shape=jax.ShapeDtypeStruct((t, width), BF16),
        scratch_shapes=[pltpu.VMEM((SUBLANES, width), F32),
                        pltpu.VMEM((tc, width), F32), pltpu.VMEM((tc, width), F32)],
        compiler_params=_cparams("parallel", "arbitrary"),
        name="lru_core",
    )(xc, w_ga.astype(BF16), vec(b_ga), w_gx.astype(BF16), vec(b_gx), vec(lam))


def _attn_kernel(q_ref, k_ref, v_ref, o_ref, lse_ref, kp_scr, vp_scr, *tile_scr, dilation, span):
    L = span
    width = q_ref.shape[2]
    qblocks = q_ref.shape[1] // L
    n = pl.program_id(1)

    @pl.when(n == 0)
    def _():
        kp_scr[...] = jnp.zeros_like(kp_scr)
        vp_scr[...] = jnp.zeros_like(vp_scr)

    qi = lax.broadcasted_iota(jnp.int32, (L, 2 * L), 0)
    ki = lax.broadcasted_iota(jnp.int32, (L, 2 * L), 1)
    dist = qi + L - ki
    in_band = (dist >= 0) & (dist <= L)
    lane_q = lax.broadcasted_iota(jnp.int32, (L, LANES), 1)
    lane_v = lax.broadcasted_iota(jnp.int32, (2 * L, LANES), 1)
    zero = jnp.zeros((), BF16)
    heads_per_pair = LANES // ATTN_HEAD_DIM

    def block(r, u):
        valid = in_band & (ki + (n * qblocks + u - 1) * L >= 0)
        lse_tile = jnp.zeros((L, LANES), F32)
        o_parts = []
        for j in range(width // LANES):
            sl = slice(j * LANES, (j + 1) * LANES)
            qp = q_ref[r, u * L:(u + 1) * L, sl]
            if u == 0:
                kk = jnp.concatenate([kp_scr[r, :, sl], k_ref[r, 0:L, sl]], axis=0)
                vv = jnp.concatenate([vp_scr[r, :, sl], v_ref[r, 0:L, sl]], axis=0)
            else:
                kk = k_ref[r, (u - 1) * L:(u + 1) * L, sl]
                vv = v_ref[r, (u - 1) * L:(u + 1) * L, sl]
            acc = jnp.zeros((L, LANES), F32)
            for hh in range(heads_per_pair):
                in_head_q = lane_q < ATTN_HEAD_DIM if hh == 0 else lane_q >= ATTN_HEAD_DIM
                in_head_v = lane_v < ATTN_HEAD_DIM if hh == 0 else lane_v >= ATTN_HEAD_DIM
                s = _dot_nt(jnp.where(in_head_q, qp, zero), kk)
                s = jnp.where(valid, s, MASK_VALUE)
                m = jnp.max(s, axis=-1, keepdims=True)
                p = jnp.exp(s - m)
                l = jnp.sum(p, axis=-1, keepdims=True)
                o = _dot(p.astype(BF16), jnp.where(in_head_v, vv, zero))
                acc = acc + o / l
                lse_tile = jnp.where(lane_q == j * heads_per_pair + hh, m + jnp.log(l), lse_tile)
            o_parts.append(acc)
        return o_parts, lse_tile

    def residue(r, carry):
        for u in range(qblocks):
            o_parts, lse_tile = block(r, u)
            if dilation == 1:
                o_ref[u * L:(u + 1) * L, :] = jnp.concatenate(o_parts, axis=1).astype(o_ref.dtype)
                lse_ref[u * L:(u + 1) * L, :] = lse_tile
            else:
                o_scr, lse_scr = tile_scr
                rows = pl.ds(u * L * dilation + r, L, stride=dilation)
                for j, part in enumerate(o_parts):
                    o_scr[j, rows, :] = part
                lse_scr[rows, :] = lse_tile
        kp_scr[r] = k_ref[r, (qblocks - 1) * L:qblocks * L, :]
        vp_scr[r] = v_ref[r, (qblocks - 1) * L:qblocks * L, :]
        return carry

    if dilation == 1:
        residue(0, 0)
    else:
        lax.fori_loop(0, dilation, residue, 0, unroll=2)
        o_scr, lse_scr = tile_scr
        for j in range(width // LANES):
            o_ref[:, j * LANES:(j + 1) * LANES] = o_scr[j].astype(o_ref.dtype)
        lse_ref[...] = lse_scr[...]


def _attn_group(qkv_g, batch, seq, dilation, span):
    width = ATTN_HEADS * ATTN_HEAD_DIM
    m = seq // dilation
    qblocks = 2 if dilation == 1 else 1
    nsteps = m // (span * qblocks)
    t = batch * seq
    rows = span * qblocks * dilation
    spec = lambda which: pl.BlockSpec((None, dilation, span * qblocks, width), lambda b, n: (b, 0, n, which))
    scratch = [pltpu.VMEM((dilation, span, width), BF16), pltpu.VMEM((dilation, span, width), BF16)]
    if dilation > 1:
        scratch += [pltpu.VMEM((width // LANES, rows, LANES), F32), pltpu.VMEM((rows, LANES), F32)]
    return pl.pallas_call(
        functools.partial(_attn_kernel, dilation=dilation, span=span),
        grid=(batch, nsteps),
        in_specs=[spec(0), spec(1), spec(2)],
        out_specs=[pl.BlockSpec((rows, width), lambda b, n: (b * nsteps + n, 0)),
                   pl.BlockSpec((rows, LANES), lambda b, n: (b * nsteps + n, 0))],
        out_shape=[jax.ShapeDtypeStruct((t, width), BF16),
                   jax.ShapeDtypeStruct((t, LANES), F32)],
        scratch_shapes=scratch,
        compiler_params=_cparams("parallel", "arbitrary"),
        name=f"attn_d{dilation}",
    )(qkv_g, qkv_g, qkv_g)


def _attn_out_kernel(o0_ref, o1_ref, o2_ref, l0_ref, l1_ref, l2_ref, w_ref, x_ref, gate_ref, gp_ref,
                     out_ref):
    tm, width = o0_ref.shape
    lane = lax.broadcasted_iota(jnp.int32, (tm, LANES), 1)
    first_head = lane < ATTN_HEAD_DIM
    lses = (l0_ref[...], l1_ref[...], l2_ref[...])
    outs = (o0_ref, o1_ref, o2_ref)
    parts = []
    for j in range(width // LANES):
        sl = slice(j * LANES, (j + 1) * LANES)
        ls = [jnp.where(first_head, l[:, 2 * j:2 * j + 1], l[:, 2 * j + 1:2 * j + 2]) for l in lses]
        mx = jnp.maximum(jnp.maximum(ls[0], ls[1]), ls[2])
        es = [jnp.exp(l - mx) for l in ls]
        num = sum(e * o[:, sl].astype(F32) for e, o in zip(es, outs))
        parts.append(num / (es[0] + es[1] + es[2]))
    o = jnp.concatenate(parts, axis=1).astype(BF16)
    out_ref[...] = _residual(x_ref[...], _dot(o, w_ref[...]), gate_ref[...], gp_ref[...])


def _attn_out(outs, lses, w, x, seq, modl, which, g_post):
    t, d = x.shape
    width = w.shape[0]
    tm = ROW_TILE
    per = seq // tm
    row = lambda n: pl.BlockSpec((tm, n), lambda i: (i, 0))
    return pl.pallas_call(
        _attn_out_kernel,
        grid=(t // tm,),
        in_specs=[row(width)] * 3 + [row(LANES)] * 3 + [
            _resident((width, d)),
            row(d),
            pl.BlockSpec((None, 1, d), lambda i: ((i // per) * N_MOD + which, 0, 0)),
            pl.BlockSpec((1, d), lambda i: (0, 0))],
        out_specs=row(d),
        out_shape=jax.ShapeDtypeStruct((t, d), F32),
        compiler_params=_cparams("parallel"),
        name="attn_out",
    )(*outs, *lses, w, x, modl, g_post.reshape(1, d))


def _rope_tables(positions):
    half = ROPE_DIM // 2
    dim = jnp.arange(LANES) % ATTN_HEAD_DIM
    inv_freq = ROPE_THETA ** (-(2 * (dim % half)).astype(F32) / ROPE_DIM)
    ang = positions.astype(F32).reshape(-1, 1) * inv_freq
    cos, sin = jnp.cos(ang), jnp.sin(ang)
    c = jnp.where(dim < ROPE_DIM, cos, 1.0)
    s1 = jnp.where(dim < half, -sin, 0.0)
    s2 = jnp.where((dim >= half) & (dim < ROPE_DIM), sin, 0.0)
    return c, s1, s2


def _attention_layer(xf, batch, seq, g_pre, g_post, modl, rope, w_qkv, w_out):
    dilations = [d for _, d in ATTN_CONFIGS]
    qkvs = _inproj_qkv(xf, seq, g_pre, modl, 0, w_qkv, rope, dilations, ATTN_HEAD_DIM ** -0.5)
    outs, lses = [], []
    for qkv_g, (window, dilation) in zip(qkvs, ATTN_CONFIGS):
        qkv_g = qkv_g.reshape(batch, dilation, seq // dilation, qkv_g.shape[-1])
        o, lse = _attn_group(qkv_g, batch, seq, dilation, window // dilation)
        outs.append(o)
        lses.append(lse)
    return _attn_out(outs, lses, w_out, xf, seq, modl, 2, g_post)


def kernel(x, c, positions, ada_w, ada_b, norm_mix_pre, norm_mix_post, norm_mlp_pre, norm_mlp_post, mlp_w1, mlp_w2, ssd_w_in, ssd_conv_w, ssd_conv_b, ssd_dt_bias, ssd_a_log, ssd_d, ssd_norm, ssd_w_out, lru_w_in, lru_conv_w, lru_conv_b, lru_w_gate_a, lru_b_gate_a, lru_w_gate_x, lru_b_gate_x, lru_lambda, lru_w_out, attn_w_qkv, attn_w_out):
    batch, seq, d = x.shape
    depth = ada_w.shape[0]
    xf = x.reshape(batch * seq, d)
    mod = _modulation(c, ada_w, ada_b)
    rope = _rope_tables(positions)
    w1_all, w2_all = mlp_w1.astype(BF16), mlp_w2.astype(BF16)
    for layer in range(depth):
        modl = mod[layer].reshape(batch * N_MOD, 1, d)
        kind, occ = layer % N_MIXERS, layer // N_MIXERS
        if kind == 0:
            w_in = ssd_w_in[occ]
            n_heads = ssd_dt_bias.shape[1]
            d_inner = n_heads * SSD_HEAD_DIM
            n_main = w_in.shape[1] - n_heads
            w_dt = jnp.pad(w_in[:, n_main:], ((0, 0), (0, LANES - n_heads))).astype(BF16)
            z_scale = jnp.where(jnp.arange(n_main) < d_inner, 0.5, 1.0)
            z, xbc, dt_raw = _inproj(xf, seq, norm_mix_pre[layer], modl, 0,
                                     (w_in[:, :n_main] * z_scale).astype(BF16), d_inner, w_dt=w_dt)
            y = _ssd_core(xbc, dt_raw, batch, seq, ssd_conv_w[occ], ssd_conv_b[occ], ssd_dt_bias[occ],
                          ssd_a_log[occ], ssd_d[occ])
            xf = _outproj(y, z, ssd_w_out[occ].astype(BF16), xf, seq, modl, 2, norm_mix_post[layer],
                          norm_g=ssd_norm[occ], n_groups=SSD_N_GROUPS)
        elif kind == 1:
            width = lru_conv_w.shape[-1]
            gate, xc = _inproj(xf, seq, norm_mix_pre[layer], modl, 0, lru_w_in[occ].astype(BF16), width,
                               conv_w=lru_conv_w[occ], conv_b=lru_conv_b[occ])
            hs = _lru_core(xc, batch, seq, lru_w_gate_a[occ], lru_b_gate_a[occ], lru_w_gate_x[occ],
                           lru_b_gate_x[occ], lru_lambda[occ])
            xf = _outproj(hs, gate, lru_w_out[occ].astype(BF16), xf, seq, modl, 2, norm_mix_post[layer])
        else:
            xf = _attention_layer(xf, batch, seq, norm_mix_pre[layer], norm_mix_post[layer], modl, rope,
                                  attn_w_qkv[occ].astype(BF16), attn_w_out[occ].astype(BF16))
        xf = _mlp(xf, seq, modl, norm_mlp_pre[layer], norm_mlp_post[layer], w1_all, w2_all, layer)
    return xf.reshape(batch, seq, d)
```

```python
import functools

import jax
import jax.numpy as jnp
from jax import lax
from jax.experimental import pallas as pl
from jax.experimental.pallas import tpu as pltpu

F32 = jnp.float32
BF16 = jnp.bfloat16

NORM_EPS = 1e-6
N_MIXERS = 3
N_MOD = 6

SSD_HEAD_DIM = 64
SSD_N_GROUPS = 8
SSD_D_STATE = 128
SSD_CHUNK = 128

LRU_N_BLOCKS = 4
LRU_C = 8.0
LRU_CHUNK = 256

ATTN_HEAD_DIM = 64
ATTN_HEADS = 8
ATTN_CONFIGS = ((128, 1), (512, 4), (2048, 16))
ROPE_THETA = 500000.0
ROPE_DIM = ATTN_HEAD_DIM // 4
MASK_VALUE = -1e30

LANES = 128
SUBLANES = 8
VMEM_LIMIT = 56 * 1024 * 1024

ROW_TILE = 1024
COL_TILE = 512


def _cparams(*sem):
    return pltpu.CompilerParams(dimension_semantics=sem, vmem_limit_bytes=VMEM_LIMIT)


def _dot(a, b):
    return jnp.dot(a, b, preferred_element_type=F32)


def _dot_nt(a, b):
    return lax.dot_general(a, b, (((1,), (1,)), ((), ())), preferred_element_type=F32)


def _split3(v):
    hi = v.astype(BF16)
    r1 = v - hi.astype(F32)
    mid = r1.astype(BF16)
    lo = (r1 - mid.astype(F32)).astype(BF16)
    return hi, mid, lo


def _rms(y):
    return y * lax.rsqrt(jnp.mean(y * y, axis=-1, keepdims=True) + NORM_EPS)


def _norm_mod(x, g, sc, sh):
    return _rms(x) * g * (1.0 + sc) + sh


def _residual(x, y, gate, g_post):
    return x + (1.0 + gate) * (_rms(y) * g_post)


def _softplus(x):
    return jnp.maximum(x, 0.0) + jnp.log1p(jnp.exp(-jnp.abs(x)))


def _sigmoid(x):
    return 0.5 + 0.5 * jnp.tanh(0.5 * x)


def _silu_of_twice(h):
    return h + h * jnp.tanh(h)


def _silu(x):
    return _silu_of_twice(0.5 * x)


CONV_TAPS = 4


PERM_CHUNK = SSD_CHUNK
PERM_SEG = PERM_CHUNK // SUBLANES
HALO_ROWS = (CONV_TAPS - 1) * SUBLANES


def _perm_time(r):
    return (r % SUBLANES) * PERM_SEG + r // SUBLANES


def _causal_conv(x, halo, w, b):
    n = PERM_CHUNK
    row8 = lax.broadcasted_iota(jnp.int32, (SUBLANES, x.shape[1]), 0)
    wrapped = []
    for k in range(CONV_TAPS - 1):
        cur = x[n - HALO_ROWS + k * SUBLANES:n - HALO_ROWS + (k + 1) * SUBLANES]
        prev = halo[k * SUBLANES:(k + 1) * SUBLANES]
        wrapped.append(jnp.where(row8 == 0, pltpu.roll(prev, 1, 0), pltpu.roll(cur, 1, 0)))
    acc = x * w[CONV_TAPS - 1:CONV_TAPS, :] + b
    for j in range(1, CONV_TAPS):
        xj = jnp.concatenate(wrapped[CONV_TAPS - 1 - j:] + [x[0:n - j * SUBLANES]], axis=0)
        acc = acc + xj * w[CONV_TAPS - 1 - j:CONV_TAPS - j, :]
    return acc


def _permute_rows(v, scr, to_time_order):
    tm = v.shape[0]
    nb = v.shape[1] // LANES
    for j in range(nb):
        scr[j] = v[:, j * LANES:(j + 1) * LANES]
    cols = []
    for j in range(nb):
        rows = []
        for c0 in range(0, tm, PERM_CHUNK):
            for a in range(PERM_CHUNK // SUBLANES):
                if to_time_order:
                    s, i0 = divmod(a * SUBLANES, PERM_SEG)
                    rows.append(scr[j, pl.ds(c0 + i0 * SUBLANES + s, SUBLANES, stride=SUBLANES), :])
                else:
                    rows.append(scr[j, pl.ds(c0 + a, SUBLANES, stride=PERM_SEG), :])
        cols.append(jnp.concatenate(rows, axis=0))
    return jnp.concatenate(cols, axis=1)


def _resident(shape):
    return pl.BlockSpec(shape, lambda *_: (0,) * len(shape), pipeline_mode=pl.Buffered(1))


def _mod_kernel(c_ref, w_ref, b_ref, o_ref):
    c = c_ref[...]
    a_hi, a_mid, _ = _split3(_silu(c))
    w = w_ref[...]
    w_hi = w.astype(BF16)
    w_lo = (w - w_hi.astype(F32)).astype(BF16)
    o_ref[...] = _dot(a_hi, w_hi) + _dot(a_hi, w_lo) + _dot(a_mid, w_hi) + b_ref[...]


def _modulation(c, ada_w, ada_b):
    depth, d, e = ada_w.shape
    b = c.shape[0]
    tn = 1024
    return pl.pallas_call(
        _mod_kernel,
        grid=(depth, e // tn),
        in_specs=[pl.BlockSpec((b, d), lambda l, j: (0, 0)),
                  pl.BlockSpec((None, d, tn), lambda l, j: (l, 0, j)),
                  pl.BlockSpec((None, 1, tn), lambda l, j: (l, 0, j))],
        out_specs=pl.BlockSpec((None, b, tn), lambda l, j: (l, 0, j)),
        out_shape=jax.ShapeDtypeStruct((depth, b, e), F32),
        compiler_params=_cparams("parallel", "parallel"),
        name="adaln_mod",
    )(c, ada_w, ada_b.reshape(depth, 1, e))


def _rope(y, rc, rs1, rs2):
    reps = y.shape[1] // LANES
    tile = lambda t: jnp.concatenate([t] * reps, axis=1)
    n = y.shape[1]
    half = ROPE_DIM // 2
    return y * tile(rc) + pltpu.roll(y, n - half, 1) * tile(rs1) + pltpu.roll(y, half, 1) * tile(rs2)


def _inproj_kernel(*refs, n_first, per, has_conv, has_dt):
    x_ref, g_ref, sc_ref, sh_ref, w_ref = refs[:5]
    pos = 5
    if has_conv:
        cw_ref, cb_ref = refs[pos:pos + 2]
        pos += 2
    if has_dt:
        wdt_ref = refs[pos]
        pos += 1
    first_ref, second_ref = refs[pos:pos + 2]
    pos += 2
    if has_dt:
        odt_ref = refs[pos]
        pos += 1
    h_scr, perm_scr = refs[pos:pos + 2]
    tm = x_ref.shape[0]
    tn = COL_TILE

    if has_conv:
        halo_scr = refs[pos + 2]

        @pl.when(pl.program_id(0) % per == 0)
        def _():
            halo_scr[...] = jnp.zeros_like(halo_scr)

    h = _norm_mod(x_ref[...], g_ref[...], sc_ref[...], sh_ref[...])
    h_scr[...] = _permute_rows(h, perm_scr, to_time_order=False).astype(BF16)
    if has_dt:
        odt_ref[...] = _dot(h_scr[...], wdt_ref[...])
    for c0 in range(0, w_ref.shape[1], tn):
        acc = _dot(h_scr[...], w_ref[:, c0:c0 + tn])
        cc = slice(c0 - n_first, c0 - n_first + tn)
        if c0 < n_first:
            first_ref[:, c0:c0 + tn] = acc.astype(BF16)
        elif has_conv:
            halo = halo_scr[:, cc]
            for r0 in range(0, tm, PERM_CHUNK):
                x_c = acc[r0:r0 + PERM_CHUNK]
                second_ref[r0:r0 + PERM_CHUNK, cc] = _causal_conv(x_c, halo, cw_ref[:, cc], cb_ref[:, cc]).astype(BF16)
                halo = x_c[PERM_CHUNK - HALO_ROWS:]
            halo_scr[:, cc] = halo
        else:
            second_ref[:, cc] = acc.astype(BF16)


def _inproj(x, seq, g, modl, which, w, n_first, conv_w=None, conv_b=None, w_dt=None):
    t, d = x.shape
    n = w.shape[1]
    n_second = n - n_first
    tm = ROW_TILE
    per = seq // tm
    has_conv = conv_w is not None
    in_specs = [pl.BlockSpec((tm, d), lambda i: (i, 0)),
                pl.BlockSpec((1, d), lambda i: (0, 0)),
                pl.BlockSpec((None, 1, d), lambda i: ((i // per) * N_MOD + which + 1, 0, 0)),
                pl.BlockSpec((None, 1, d), lambda i: ((i // per) * N_MOD + which, 0, 0)),
                _resident((d, n))]
    args = [x, g.reshape(1, d), modl, modl, w]
    scratch = [pltpu.VMEM((tm, d), BF16), pltpu.VMEM((d // LANES, tm, LANES), F32)]
    if has_conv:
        in_specs += [_resident((CONV_TAPS, n_second)), _resident((1, n_second))]
        args += [conv_w, conv_b.reshape(1, n_second)]
        scratch.append(pltpu.VMEM((HALO_ROWS, n_second), F32))
    out_specs = [pl.BlockSpec((tm, n_first), lambda i: (i, 0)), pl.BlockSpec((tm, n_second), lambda i: (i, 0))]
    out_shape = [jax.ShapeDtypeStruct((t, n_first), BF16), jax.ShapeDtypeStruct((t, n_second), BF16)]
    if w_dt is not None:
        in_specs.append(_resident((d, LANES)))
        args.append(w_dt)
        out_specs.append(pl.BlockSpec((tm, LANES), lambda i: (i, 0)))
        out_shape.append(jax.ShapeDtypeStruct((t, LANES), F32))
    return pl.pallas_call(
        functools.partial(_inproj_kernel, n_first=n_first, per=per, has_conv=has_conv, has_dt=w_dt is not None),
        grid=(t // tm,),
        in_specs=in_specs,
        out_specs=out_specs,
        out_shape=out_shape,
        scratch_shapes=scratch,
        compiler_params=_cparams("arbitrary" if has_conv else "parallel"),
        name="inproj",
    )(*args)


def _inproj_qkv_kernel(x_ref, g_ref, sc_ref, sh_ref, w_ref, rc_ref, rs1_ref, rs2_ref, *refs, dilations,
                       q_scale):
    out_refs = refs[:len(dilations)]
    h_scr, tile_scr = refs[len(dilations):]
    tm = x_ref.shape[0]
    tn = COL_TILE

    h_scr[...] = _norm_mod(x_ref[...], g_ref[...], sc_ref[...], sh_ref[...]).astype(BF16)
    for c0 in range(0, w_ref.shape[1], tn):
        acc = _dot(h_scr[...], w_ref[:, c0:c0 + tn])
        grp, kind = divmod(c0 // tn, 3)
        if kind < 2:
            acc = _rope(acc, rc_ref[...], rs1_ref[...], rs2_ref[...])
            if kind == 0:
                acc = acc * q_scale
        o_ref, d = out_refs[grp], dilations[grp]
        if d == 1:
            o_ref[0, :, kind * tn:(kind + 1) * tn] = acc.astype(BF16)
        else:
            for j in range(tn // LANES):
                tile_scr[j] = acc[:, j * LANES:(j + 1) * LANES]
            for r in range(d):
                for j in range(tn // LANES):
                    o_ref[r, :, kind * tn + j * LANES:kind * tn + (j + 1) * LANES] = (
                        tile_scr[j, pl.ds(r, tm // d, stride=d), :].astype(BF16))


def _inproj_qkv(x, seq, g, modl, which, w, rope, dilations, q_scale):
    t, d = x.shape
    n = w.shape[1]
    tm = ROW_TILE
    per = seq // tm
    batch = t // seq
    gw = n // len(dilations)
    in_specs = [pl.BlockSpec((tm, d), lambda i: (i, 0)),
                pl.BlockSpec((1, d), lambda i: (0, 0)),
                pl.BlockSpec((None, 1, d), lambda i: ((i // per) * N_MOD + which + 1, 0, 0)),
                pl.BlockSpec((None, 1, d), lambda i: ((i // per) * N_MOD + which, 0, 0)),
                _resident((d, n))] + [pl.BlockSpec((tm, LANES), lambda i: (i, 0))] * 3
    return pl.pallas_call(
        functools.partial(_inproj_qkv_kernel, dilations=tuple(dilations), q_scale=q_scale),
        grid=(t // tm,),
        in_specs=in_specs,
        out_specs=[pl.BlockSpec((None, dl, None, tm // dl, gw), lambda i: (i // per, 0, i % per, 0, 0))
                   for dl in dilations],
        out_shape=[jax.ShapeDtypeStruct((batch, dl, per, tm // dl, gw), BF16) for dl in dilations],
        scratch_shapes=[pltpu.VMEM((tm, d), BF16), pltpu.VMEM((COL_TILE // LANES, tm, LANES), F32)],
        compiler_params=_cparams("parallel"),
        name="inproj_qkv",
    )(x, g.reshape(1, d), modl, modl, w, *rope)


def _outproj_kernel(y_ref, z_ref, *refs, n_groups):
    if n_groups:
        ng_ref, *refs = refs
    w_ref, x_ref, gate_ref, gp_ref, o_ref, perm_scr = refs
    y = y_ref[...].astype(F32)
    z = z_ref[...].astype(F32)
    if n_groups:
        y = y * _silu_of_twice(z)
        gw = y.shape[1] // n_groups
        y = jnp.concatenate([_rms(y[:, g * gw:(g + 1) * gw]) for g in range(n_groups)], axis=1) * ng_ref[...]
    else:
        y = y * jax.nn.gelu(z, approximate=True)
    out = _permute_rows(_dot(y.astype(BF16), w_ref[...]), perm_scr, to_time_order=True)
    o_ref[...] = _residual(x_ref[...], out, gate_ref[...], gp_ref[...])


def _outproj(y, z, w, x, seq, modl, which, g_post, norm_g=None, n_groups=0):
    t, kdim = y.shape
    d = w.shape[1]
    tm = ROW_TILE
    per = seq // tm
    row = pl.BlockSpec((tm, kdim), lambda i: (i, 0))
    in_specs = [row, row]
    args = [y, z]
    if n_groups:
        in_specs.append(pl.BlockSpec((1, kdim), lambda i: (0, 0)))
        args.append(norm_g.reshape(1, kdim))
    in_specs += [_resident((kdim, d)),
                 pl.BlockSpec((tm, d), lambda i: (i, 0)),
                 pl.BlockSpec((None, 1, d), lambda i: ((i // per) * N_MOD + which, 0, 0)),
                 pl.BlockSpec((1, d), lambda i: (0, 0))]
    return pl.pallas_call(
        functools.partial(_outproj_kernel, n_groups=n_groups),
        grid=(t // tm,),
        in_specs=in_specs,
        out_specs=pl.BlockSpec((tm, d), lambda i: (i, 0)),
        out_shape=jax.ShapeDtypeStruct((t, d), F32),
        scratch_shapes=[pltpu.VMEM((d // LANES, tm, LANES), F32)],
        compiler_params=_cparams("parallel"),
        name="outproj",
    )(*args, w, x, modl, g_post.reshape(1, d))


def _mlp_kernel(x_ref, g_ref, sc_ref, sh_ref, w1_ref, w2_ref, gate_ref, gp_ref, o_ref, h_scr):
    h_scr[...] = _norm_mod(x_ref[...], g_ref[...], sc_ref[...], sh_ref[...]).astype(BF16)
    tf = COL_TILE
    acc = None
    for f0 in range(0, w1_ref.shape[1], tf):
        a = jnp.maximum(_dot(h_scr[...], w1_ref[:, f0:f0 + tf]), 0.0)
        part = _dot((a * a).astype(BF16), w2_ref[f0:f0 + tf, :])
        acc = part if acc is None else acc + part
    o_ref[...] = _residual(x_ref[...], acc, gate_ref[...], gp_ref[...])


def _mlp(x, seq, modl, g_pre, g_post, w1, w2, layer):
    t, d = x.shape
    dff = w1.shape[2]
    tm = ROW_TILE
    per = seq // tm
    mod_spec = lambda which: pl.BlockSpec((None, 1, d), lambda i: ((i // per) * N_MOD + which, 0, 0))
    slab = lambda r, c: pl.BlockSpec((None, r, c), lambda i: (layer, 0, 0), pipeline_mode=pl.Buffered(1))
    return pl.pallas_call(
        _mlp_kernel,
        grid=(t // tm,),
        in_specs=[pl.BlockSpec((tm, d), lambda i: (i, 0)),
                  pl.BlockSpec((1, d), lambda i: (0, 0)),
                  mod_spec(4), mod_spec(3),
                  slab(d, dff), slab(dff, d),
                  mod_spec(5),
                  pl.BlockSpec((1, d), lambda i: (0, 0))],
        out_specs=pl.BlockSpec((tm, d), lambda i: (i, 0)),
        out_shape=jax.ShapeDtypeStruct((t, d), F32),
        scratch_shapes=[pltpu.VMEM((tm, d), BF16)],
        compiler_params=_cparams("parallel"),
        name="mlp",
    )(x, g_pre.reshape(1, d), modl, modl, w1, w2, modl, g_post.reshape(1, d))


def _ssd_decays(dt_raw, dt_bias, a_log_col, n_heads, cum):
    L = dt_raw.shape[0]
    dt_t = _softplus((dt_raw + dt_bias).T[0:n_heads])
    a_col = -jnp.exp(a_log_col[0:n_heads, :])
    a_cs_t = sum(_dot(term, cum) for term in _split3(dt_t * a_col))
    e_cs_t = jnp.exp(a_cs_t)
    dte_t = jnp.exp(a_cs_t[:, L - 1:L] - a_cs_t) * dt_t
    to_cols = lambda v: jnp.concatenate([v, jnp.zeros((LANES - n_heads, L), F32)], axis=0).T
    e_hi, e_mid, _ = _split3(to_cols(e_cs_t))
    d_hi, d_mid, _ = _split3(to_cols(dte_t))
    return to_cols(a_cs_t), a_cs_t - jnp.log(dt_t), e_hi, e_mid, d_hi, d_mid


def _ssd_kernel(xbc_ref, dt_ref, dtn_ref, cum_ref, cw_ref, cb_ref, dtb_ref, alog_ref, dsk_ref, exp_ref, y_ref,
                state_scr, tail_scr, act_scr, *decay_scr, d_inner):
    L, N, G = SSD_CHUNK, SSD_D_STATE, SSD_N_GROUPS
    H = d_inner // SSD_HEAD_DIM
    gw = d_inner // G
    conv_dim = d_inner + 2 * G * N
    c = pl.program_id(1)

    @pl.when(c == 0)
    def _():
        state_scr[...] = jnp.zeros_like(state_scr)
        tail_scr[...] = jnp.zeros_like(tail_scr)
        for scr, v in zip(decay_scr, _ssd_decays(dt_ref[...], dtb_ref[...], alog_ref[...], H, cum_ref[...])):
            scr[...] = v

    a_cs, src_t, e_hi, e_mid, d_hi, d_mid = [scr[...] for scr in decay_scr]
    nxt = _ssd_decays(dtn_ref[...], dtb_ref[...], alog_ref[...], H, cum_ref[...])

    strip = 512
    for s0 in range(0, conv_dim, strip):
        cols = slice(s0, s0 + strip)
        xin = xbc_ref[:, cols].astype(F32)
        act_scr[:, cols] = _silu_of_twice(_causal_conv(xin, tail_scr[:, cols], cw_ref[:, cols], cb_ref[:, cols]))
        tail_scr[:, cols] = xin[L - HALO_ROWS:L]

    row = lax.broadcasted_iota(jnp.int32, (L, L), 0)
    col = lax.broadcasted_iota(jnp.int32, (L, L), 1)
    tri = _perm_time(row) >= _perm_time(col)
    act = lambda lo, hi: act_scr[:, lo:hi]
    lane = lax.broadcasted_iota(jnp.int32, (L, LANES), 1)
    first_head = lane < SSD_HEAD_DIM
    heads_per_pair = LANES // SSD_HEAD_DIM
    for g in range(G):
        gs = slice(g * gw, (g + 1) * gw)
        expand = exp_ref[:, gs]
        e_cs_x = _dot(e_hi, expand) + _dot(e_mid, expand)
        dte_x = _dot(d_hi, expand) + _dot(d_mid, expand)
        xs_g = act(g * gw, (g + 1) * gw)
        bm_g = act(d_inner + g * N, d_inner + (g + 1) * N)
        cm_g = act(d_inner + (G + g) * N, d_inner + (G + g + 1) * N).astype(BF16)
        cb16 = _dot_nt(cm_g, bm_g.astype(BF16)).astype(BF16)
        st_prev = state_scr[g]
        y_off = _dot(cm_g, st_prev.astype(BF16)) * e_cs_x
        w_g = (xs_g * dte_x).astype(BF16)
        new_state = _dot(bm_g.T.astype(BF16), w_g)
        state_scr[g] = st_prev * e_cs_x[L - 1:L, :] + new_state
        y_parts = []
        for q in range(gw // LANES):
            xs_p = xs_g[:, q * LANES:(q + 1) * LANES]
            ms = []
            for hh in range(heads_per_pair):
                h = (g * gw + q * LANES) // SSD_HEAD_DIM + hh
                seg = a_cs[:, h:h + 1] - src_t[h:h + 1, :]
                ms.append(cb16 * jnp.exp(jnp.where(tri, seg, -jnp.inf).astype(BF16)))
            m_cat = jnp.concatenate(ms, axis=1)
            x_cat = jnp.concatenate([jnp.where(first_head, xs_p, 0.0),
                                     jnp.where(first_head, 0.0, xs_p)], axis=0).astype(BF16)
            y_parts.append(_dot(m_cat, x_cat))
        y = jnp.concatenate(y_parts, axis=1) + y_off + xs_g * dsk_ref[:, gs]
        y_ref[:, gs] = y.astype(y_ref.dtype)

    for scr, v in zip(decay_scr, nxt):
        scr[...] = v


def _ssd_core(xbc, dt_raw, batch, seq, conv_w, conv_b, dt_bias, a_log, d_skip):
    t, conv_dim = xbc.shape
    n_heads = dt_bias.shape[0]
    d_inner = n_heads * SSD_HEAD_DIM
    L = SSD_CHUNK
    nc = seq // L
    pad = lambda v: jnp.pad(v.astype(F32), (0, LANES - n_heads))
    expand = (jnp.arange(LANES)[:, None] == (jnp.arange(d_inner)[None, :] // SSD_HEAD_DIM)).astype(BF16)
    dsk_x = jnp.repeat(d_skip.astype(F32), SSD_HEAD_DIM).reshape(1, d_inner)
    when = _perm_time(jnp.arange(L))
    cum = (when[:, None] <= when[None, :]).astype(BF16)
    return pl.pallas_call(
        functools.partial(_ssd_kernel, d_inner=d_inner),
        grid=(batch, nc),
        in_specs=[pl.BlockSpec((L, conv_dim), lambda b, c: (b * nc + c, 0)),
                  pl.BlockSpec((L, LANES), lambda b, c: (b * nc + c, 0)),
                  pl.BlockSpec((L, LANES), lambda b, c: (b * nc + jnp.minimum(c + 1, nc - 1), 0)),
                  _resident((L, L)),
                  _resident((CONV_TAPS, conv_dim)), _resident((1, conv_dim)),
                  _resident((1, LANES)), _resident((LANES, 1)), _resident((1, d_inner)),
                  _resident((LANES, d_inner))],
        out_specs=pl.BlockSpec((L, d_inner), lambda b, c: (b * nc + c, 0)),
        out_shape=jax.ShapeDtypeStruct((t, d_inner), BF16),
        scratch_shapes=[pltpu.VMEM((SSD_N_GROUPS, SSD_D_STATE, d_inner // SSD_N_GROUPS), F32),
                        pltpu.VMEM((HALO_ROWS, conv_dim), F32),
                        pltpu.VMEM((L, conv_dim), F32),
                        pltpu.VMEM((L, LANES), F32), pltpu.VMEM((n_heads, L), F32)]
        + [pltpu.VMEM((L, LANES), BF16)] * 4,
        compiler_params=_cparams("parallel", "arbitrary"),
        name="ssd_core",
    )(xbc, dt_raw, dt_raw, cum, 0.5 * conv_w, 0.5 * conv_b.reshape(1, conv_dim), pad(dt_bias).reshape(1, LANES),
      pad(a_log).reshape(LANES, 1), dsk_x, expand)


def _lru_kernel(xc_ref, wga_ref, bga_ref, wgx_ref, bgx_ref, lam_ref, y_ref, h_scr, a_scr, u_scr):
    tc, width = y_ref.shape
    blk = width // LRU_N_BLOCKS

    @pl.when(pl.program_id(1) == 0)
    def _():
        h_scr[...] = jnp.zeros_like(h_scr)

    neg_sp = _softplus(-lam_ref[...])
    for k in range(LRU_N_BLOCKS):
        sl = slice(k * blk, (k + 1) * blk)
        xb16 = xc_ref[:, sl]
        r = _sigmoid(_dot(xb16, wga_ref[k]) + bga_ref[:, sl])
        i = _sigmoid(_dot(xb16, wgx_ref[k]) + bgx_ref[:, sl])
        log_a = -LRU_C * r * neg_sp[:, sl]
        a_scr[:, sl] = jnp.exp(log_a)
        u_scr[:, sl] = jnp.sqrt(1.0 - jnp.exp(2.0 * log_a)) * (i * xb16.astype(F32))

    row = lax.broadcasted_iota(jnp.int32, (SUBLANES, width), 0)
    h_in = h_scr[...]
    for c0 in range(0, tc, PERM_CHUNK):
        vreg = lambda i: pl.ds(c0 + i * SUBLANES, SUBLANES)
        for i in range(1, PERM_SEG):
            a_i = a_scr[vreg(i), :]
            u_scr[vreg(i), :] = a_i * u_scr[vreg(i - 1), :] + u_scr[vreg(i), :]
            a_scr[vreg(i), :] = a_i * a_scr[vreg(i - 1), :]
        a = a_scr[vreg(PERM_SEG - 1), :]
        u = u_scr[vreg(PERM_SEG - 1), :]
        for k in (1, 2, 4):
            keep = row >= k
            a_sh = jnp.where(keep, pltpu.roll(a, k, 0), 1.0)
            u_sh = jnp.where(keep, pltpu.roll(u, k, 0), 0.0)
            u = a * u_sh + u
            a = a * a_sh
        carry = jnp.where(row == 0, h_in, pltpu.roll(a, 1, 0) * h_in + pltpu.roll(u, 1, 0))
        for i in range(PERM_SEG):
            u_scr[vreg(i), :] = u_scr[vreg(i), :] + a_scr[vreg(i), :] * carry
        last = u_scr[vreg(PERM_SEG - 1), :]
        h_in = jnp.broadcast_to(last[SUBLANES - 1:SUBLANES, :], last.shape)
    h_scr[...] = h_in
    y_ref[...] = u_scr[...].astype(y_ref.dtype)


def _lru_core(xc, batch, seq, w_ga, b_ga, w_gx, b_gx, lam):
    t, width = xc.shape
    tc = LRU_CHUNK
    nc = seq // tc
    vec = lambda v: v.astype(F32).reshape(1, width)
    return pl.pallas_call(
        _lru_kernel,
        grid=(batch, nc),
        in_specs=[pl.BlockSpec((tc, width), lambda b, c: (b * nc + c, 0)),
                  _resident(w_ga.shape), _resident((1, width)), _resident(w_gx.shape),
                  _resident((1, width)), _resident((1, width))],
        out_specs=pl.BlockSpec((tc, width), lambda b, c: (b * nc + c, 0)),
        out_shape=jax.ShapeDtypeStruct((t, width), BF16),
        scratch_shapes=[pltpu.VMEM((SUBLANES, width), F32),
                        pltpu.VMEM((tc, width), F32), pltpu.VMEM((tc, width), F32)],
        compiler_params=_cparams("parallel", "arbitrary"),
        name="lru_core",
    )(xc, w_ga.astype(BF16), vec(b_ga), w_gx.astype(BF16), vec(b_gx), vec(lam))


def _attn_kernel(q_ref, k_ref, v_ref, o_ref, lse_ref, kp_scr, vp_scr, *tile_scr, dilation, span):
    L = span
    width = q_ref.shape[2]
    qblocks = q_ref.shape[1] // L
    n = pl.program_id(1)

    @pl.when(n == 0)
    def _():
        kp_scr[...] = jnp.zeros_like(kp_scr)
        vp_scr[...] = jnp.zeros_like(vp_scr)

    qi = lax.broadcasted_iota(jnp.int32, (L, 2 * L), 0)
    ki = lax.broadcasted_iota(jnp.int32, (L, 2 * L), 1)
    dist = qi + L - ki
    in_band = (dist >= 0) & (dist <= L)
    lane_q = lax.broadcasted_iota(jnp.int32, (L, LANES), 1)
    lane_v = lax.broadcasted_iota(jnp.int32, (2 * L, LANES), 1)
    zero = jnp.zeros((), BF16)
    heads_per_pair = LANES // ATTN_HEAD_DIM

    def block(r, u):
        valid = in_band & (ki + (n * qblocks + u - 1) * L >= 0)
        lse_tile = jnp.zeros((L, LANES), F32)
        o_parts = []
        for j in range(width // LANES):
            sl = slice(j * LANES, (j + 1) * LANES)
            qp = q_ref[r, u * L:(u + 1) * L, sl]
            if u == 0:
                kk = jnp.concatenate([kp_scr[r, :, sl], k_ref[r, 0:L, sl]], axis=0)
                vv = jnp.concatenate([vp_scr[r, :, sl], v_ref[r, 0:L, sl]], axis=0)
            else:
                kk = k_ref[r, (u - 1) * L:(u + 1) * L, sl]
                vv = v_ref[r, (u - 1) * L:(u + 1) * L, sl]
            acc = jnp.zeros((L, LANES), F32)
            for hh in range(heads_per_pair):
                in_head_q = lane_q < ATTN_HEAD_DIM if hh == 0 else lane_q >= ATTN_HEAD_DIM
                in_head_v = lane_v < ATTN_HEAD_DIM if hh == 0 else lane_v >= ATTN_HEAD_DIM
                s = _dot_nt(jnp.where(in_head_q, qp, zero), kk)
                s = jnp.where(valid, s, MASK_VALUE)
                m = jnp.max(s, axis=-1, keepdims=True)
                p = jnp.exp(s - m)
                l = jnp.sum(p, axis=-1, keepdims=True)
                o = _dot(p.astype(BF16), jnp.where(in_head_v, vv, zero))
                acc = acc + o / l
                lse_tile = jnp.where(lane_q == j * heads_per_pair + hh, m + jnp.log(l), lse_tile)
            o_parts.append(acc)
        return o_parts, lse_tile

    def residue(r, carry):
        for u in range(qblocks):
            o_parts, lse_tile = block(r, u)
            if dilation == 1:
                o_ref[u * L:(u + 1) * L, :] = jnp.concatenate(o_parts, axis=1).astype(o_ref.dtype)
                lse_ref[u * L:(u + 1) * L, :] = lse_tile
            else:
                o_scr, lse_scr = tile_scr
                rows = pl.ds(u * L * dilation + r, L, stride=dilation)
                for j, part in enumerate(o_parts):
                    o_scr[j, rows, :] = part
                lse_scr[rows, :] = lse_tile
        kp_scr[r] = k_ref[r, (qblocks - 1) * L:qblocks * L, :]
        vp_scr[r] = v_ref[r, (qblocks - 1) * L:qblocks * L, :]
        return carry

    if dilation == 1:
        residue(0, 0)
    else:
        lax.fori_loop(0, dilation, residue, 0, unroll=2)
        o_scr, lse_scr = tile_scr
        for j in range(width // LANES):
            o_ref[:, j * LANES:(j + 1) * LANES] = o_scr[j].astype(o_ref.dtype)
        lse_ref[...] = lse_scr[...]


def _attn_group(qkv_g, batch, seq, dilation, span):
    width = ATTN_HEADS * ATTN_HEAD_DIM
    m = seq // dilation
    qblocks = 2 if dilation == 1 else 1
    nsteps = m // (span * qblocks)
    t = batch * seq
    rows = span * qblocks * dilation
    spec = lambda which: pl.BlockSpec((None, dilation, span * qblocks, width), lambda b, n: (b, 0, n, which))
    scratch = [pltpu.VMEM((dilation, span, width), BF16), pltpu.VMEM((dilation, span, width), BF16)]
    if dilation > 1:
        scratch += [pltpu.VMEM((width // LANES, rows, LANES), F32), pltpu.VMEM((rows, LANES), F32)]
    return pl.pallas_call(
        functools.partial(_attn_kernel, dilation=dilation, span=span),
        grid=(batch, nsteps),
        in_specs=[spec(0), spec(1), spec(2)],
        out_specs=[pl.BlockSpec((rows, width), lambda b, n: (b * nsteps + n, 0)),
                   pl.BlockSpec((rows, LANES), lambda b, n: (b * nsteps + n, 0))],
        out_shape=[jax.ShapeDtypeStruct((t, width), BF16),
                   jax.ShapeDtypeStruct((t, LANES), F32)],
        scratch_shapes=scratch,
        compiler_params=_cparams("parallel", "arbitrary"),
        name=f"attn_d{dilation}",
    )(qkv_g, qkv_g, qkv_g)


def _attn_out_kernel(o0_ref, o1_ref, o2_ref, l0_ref, l1_ref, l2_ref, w_ref, x_ref, gate_ref, gp_ref,
                     out_ref):
    tm, width = o0_ref.shape
    lane = lax.broadcasted_iota(jnp.int32, (tm, LANES), 1)
    first_head = lane < ATTN_HEAD_DIM
    lses = (l0_ref[...], l1_ref[...], l2_ref[...])
    outs = (o0_ref, o1_ref, o2_ref)
    parts = []
    for j in range(width // LANES):
        sl = slice(j * LANES, (j + 1) * LANES)
        ls = [jnp.where(first_head, l[:, 2 * j:2 * j + 1], l[:, 2 * j + 1:2 * j + 2]) for l in lses]
        mx = jnp.maximum(jnp.maximum(ls[0], ls[1]), ls[2])
        es = [jnp.exp(l - mx) for l in ls]
        num = sum(e * o[:, sl].astype(F32) for e, o in zip(es, outs))
        parts.append(num / (es[0] + es[1] + es[2]))
    o = jnp.concatenate(parts, axis=1).astype(BF16)
    out_ref[...] = _residual(x_ref[...], _dot(o, w_ref[...]), gate_ref[...], gp_ref[...])


def _attn_out(outs, lses, w, x, seq, modl, which, g_post):
    t, d = x.shape
    width = w.shape[0]
    tm = ROW_TILE
    per = seq // tm
    row = lambda n: pl.BlockSpec((tm, n), lambda i: (i, 0))
    return pl.pallas_call(
        _attn_out_kernel,
        grid=(t // tm,),
        in_specs=[row(width)] * 3 + [row(LANES)] * 3 + [
            _resident((width, d)),
            row(d),
            pl.BlockSpec((None, 1, d), lambda i: ((i // per) * N_MOD + which, 0, 0)),
            pl.BlockSpec((1, d), lambda i: (0, 0))],
        out_specs=row(d),
        out_shape=jax.ShapeDtypeStruct((t, d), F32),
        compiler_params=_cparams("parallel"),
        name="attn_out",
    )(*outs, *lses, w, x, modl, g_post.reshape(1, d))


def _rope_tables(positions):
    half = ROPE_DIM // 2
    inv_freq = ROPE_THETA ** (-jnp.arange(0, ROPE_DIM, 2, dtype=F32) / ROPE_DIM)
    ang = positions.astype(F32).reshape(-1, 1) * inv_freq
    cos, sin = jnp.cos(ang), jnp.sin(ang)
    dim = jnp.arange(LANES) % ATTN_HEAD_DIM
    freq = jnp.arange(half)[:, None]
    spread = lambda v, lanes: jnp.dot(v, ((dim[None, :] % half == freq) & lanes[None, :]).astype(F32),
                                      precision=lax.Precision.HIGHEST)
    c = spread(cos, dim < ROPE_DIM) + (dim >= ROPE_DIM).astype(F32)
    s1 = spread(-sin, dim < half)
    s2 = spread(sin, (dim >= half) & (dim < ROPE_DIM))
    return c, s1, s2


def _attention_layer(xf, batch, seq, g_pre, g_post, modl, rope, w_qkv, w_out):
    dilations = [d for _, d in ATTN_CONFIGS]
    qkvs = _inproj_qkv(xf, seq, g_pre, modl, 0, w_qkv, rope, dilations, ATTN_HEAD_DIM ** -0.5)
    outs, lses = [], []
    for qkv_g, (window, dilation) in zip(qkvs, ATTN_CONFIGS):
        qkv_g = qkv_g.reshape(batch, dilation, seq // dilation, qkv_g.shape[-1])
        o, lse = _attn_group(qkv_g, batch, seq, dilation, window // dilation)
        outs.append(o)
        lses.append(lse)
    return _attn_out(outs, lses, w_out, xf, seq, modl, 2, g_post)


def kernel(x, c, positions, ada_w, ada_b, norm_mix_pre, norm_mix_post, norm_mlp_pre, norm_mlp_post, mlp_w1, mlp_w2, ssd_w_in, ssd_conv_w, ssd_conv_b, ssd_dt_bias, ssd_a_log, ssd_d, ssd_norm, ssd_w_out, lru_w_in, lru_conv_w, lru_conv_b, lru_w_gate_a, lru_b_gate_a, lru_w_gate_x, lru_b_gate_x, lru_lambda, lru_w_out, attn_w_qkv, attn_w_out):
    batch, seq, d = x.shape
    depth = ada_w.shape[0]
    xf = x.reshape(batch * seq, d)
    mod = _modulation(c, ada_w, ada_b)
    rope = _rope_tables(positions)
    w1_all, w2_all = mlp_w1.astype(BF16), mlp_w2.astype(BF16)
    for layer in range(depth):
        modl = mod[layer].reshape(batch * N_MOD, 1, d)
        kind, occ = layer % N_MIXERS, layer // N_MIXERS
        if kind == 0:
            w_in = ssd_w_in[occ]
            n_heads = ssd_dt_bias.shape[1]
            d_inner = n_heads * SSD_HEAD_DIM
            n_main = w_in.shape[1] - n_heads
            w_dt = jnp.pad(w_in[:, n_main:], ((0, 0), (0, LANES - n_heads))).astype(BF16)
            z_scale = jnp.where(jnp.arange(n_main) < d_inner, 0.5, 1.0)
            z, xbc, dt_raw = _inproj(xf, seq, norm_mix_pre[layer], modl, 0,
                                     (w_in[:, :n_main] * z_scale).astype(BF16), d_inner, w_dt=w_dt)
            y = _ssd_core(xbc, dt_raw, batch, seq, ssd_conv_w[occ], ssd_conv_b[occ], ssd_dt_bias[occ],
                          ssd_a_log[occ], ssd_d[occ])
            xf = _outproj(y, z, ssd_w_out[occ].astype(BF16), xf, seq, modl, 2, norm_mix_post[layer],
                          norm_g=ssd_norm[occ], n_groups=SSD_N_GROUPS)
        elif kind == 1:
            width = lru_conv_w.shape[-1]
            gate, xc = _inproj(xf, seq, norm_mix_pre[layer], modl, 0, lru_w_in[occ].astype(BF16), width,
                               conv_w=lru_conv_w[occ], conv_b=lru_conv_b[occ])
            hs = _lru_core(xc, batch, seq, lru_w_gate_a[occ], lru_b_gate_a[occ], lru_w_gate_x[occ],
                           lru_b_gate_x[occ], lru_lambda[occ])
            xf = _outproj(hs, gate, lru_w_out[occ].astype(BF16), xf, seq, modl, 2, norm_mix_post[layer])
        else:
            xf = _attention_layer(xf, batch, seq, norm_mix_pre[layer], norm_mix_post[layer], modl, rope,
                                  attn_w_qkv[occ].astype(BF16), attn_w_out[occ].astype(BF16))
        xf = _mlp(xf, seq, modl, norm_mlp_pre[layer], norm_mlp_post[layer], w1_all, w2_all, layer)
    return xf.reshape(batch, seq, d)
```

```python
import functools

import jax
import jax.numpy as jnp
from jax import lax
from jax.experimental import pallas as pl
from jax.experimental.pallas import tpu as pltpu

F32 = jnp.float32
BF16 = jnp.bfloat16

NORM_EPS = 1e-6
N_MIXERS = 3
N_MOD = 6

SSD_HEAD_DIM = 64
SSD_N_GROUPS = 8
SSD_D_STATE = 128
SSD_CHUNK = 128

LRU_N_BLOCKS = 4
LRU_C = 8.0
LRU_CHUNK = 512

ATTN_HEAD_DIM = 64
ATTN_HEADS = 8
ATTN_CONFIGS = ((128, 1), (512, 4), (2048, 16))
ROPE_THETA = 500000.0
ROPE_DIM = ATTN_HEAD_DIM // 4
MASK_VALUE = -1e30

LANES = 128
SUBLANES = 8
VMEM_LIMIT = 56 * 1024 * 1024

ROW_TILE = 1024
COL_TILE = 512


def _cparams(*sem):
    return pltpu.CompilerParams(dimension_semantics=sem, vmem_limit_bytes=VMEM_LIMIT)


def _dot(a, b):
    return jnp.dot(a, b, preferred_element_type=F32)


def _dot_nt(a, b):
    return lax.dot_general(a, b, (((1,), (1,)), ((), ())), preferred_element_type=F32)


def _split3(v):
    hi = v.astype(BF16)
    r1 = v - hi.astype(F32)
    mid = r1.astype(BF16)
    lo = (r1 - mid.astype(F32)).astype(BF16)
    return hi, mid, lo


def _rms(y):
    return y * lax.rsqrt(jnp.mean(y * y, axis=-1, keepdims=True) + NORM_EPS)


def _norm_mod(x, g, sc, sh):
    return _rms(x) * g * (1.0 + sc) + sh


def _residual(x, y, gate, g_post):
    return x + (1.0 + gate) * (_rms(y) * g_post)


def _softplus(x):
    return jnp.maximum(x, 0.0) + jnp.log1p(jnp.exp(-jnp.abs(x)))


def _sigmoid(x):
    return 0.5 + 0.5 * jnp.tanh(0.5 * x)


def _silu_of_twice(h):
    return h + h * jnp.tanh(h)


def _silu(x):
    return _silu_of_twice(0.5 * x)


CONV_TAPS = 4


PERM_CHUNK = SSD_CHUNK
PERM_SEG = PERM_CHUNK // SUBLANES
HALO_ROWS = (CONV_TAPS - 1) * SUBLANES


def _perm_time(r):
    return (r % SUBLANES) * PERM_SEG + r // SUBLANES


def _causal_conv(x, halo, w, b):
    n = PERM_CHUNK
    row8 = lax.broadcasted_iota(jnp.int32, (SUBLANES, x.shape[1]), 0)
    wrapped = []
    for k in range(CONV_TAPS - 1):
        cur = x[n - HALO_ROWS + k * SUBLANES:n - HALO_ROWS + (k + 1) * SUBLANES]
        prev = halo[k * SUBLANES:(k + 1) * SUBLANES]
        wrapped.append(jnp.where(row8 == 0, pltpu.roll(prev, 1, 0), pltpu.roll(cur, 1, 0)))
    acc = x * w[CONV_TAPS - 1:CONV_TAPS, :] + b
    for j in range(1, CONV_TAPS):
        xj = jnp.concatenate(wrapped[CONV_TAPS - 1 - j:] + [x[0:n - j * SUBLANES]], axis=0)
        acc = acc + xj * w[CONV_TAPS - 1 - j:CONV_TAPS - j, :]
    return acc


def _permute_rows(v, scr, to_time_order):
    tm = v.shape[0]
    nb = v.shape[1] // LANES
    for j in range(nb):
        scr[j] = v[:, j * LANES:(j + 1) * LANES]
    cols = []
    for j in range(nb):
        rows = []
        for c0 in range(0, tm, PERM_CHUNK):
            for a in range(PERM_CHUNK // SUBLANES):
                if to_time_order:
                    s, i0 = divmod(a * SUBLANES, PERM_SEG)
                    rows.append(scr[j, pl.ds(c0 + i0 * SUBLANES + s, SUBLANES, stride=SUBLANES), :])
                else:
                    rows.append(scr[j, pl.ds(c0 + a, SUBLANES, stride=PERM_SEG), :])
        cols.append(jnp.concatenate(rows, axis=0))
    return jnp.concatenate(cols, axis=1)


def _resident(shape):
    return pl.BlockSpec(shape, lambda *_: (0,) * len(shape), pipeline_mode=pl.Buffered(1))


def _mod_kernel(c_ref, w_ref, b_ref, o_ref):
    @pl.when(pl.program_id(1) == 0)
    def _():
        o_ref[...] = jnp.broadcast_to(b_ref[...], o_ref.shape)

    a_hi, a_mid, _ = _split3(_silu(c_ref[...]))
    w = w_ref[...]
    w_hi = w.astype(BF16)
    w_lo = (w - w_hi.astype(F32)).astype(BF16)
    o_ref[...] += _dot(a_hi, w_hi) + _dot(a_hi, w_lo) + _dot(a_mid, w_hi)


def _modulation(c, ada_w, ada_b):
    depth, d, e = ada_w.shape
    b = c.shape[0]
    tk = 256
    return pl.pallas_call(
        _mod_kernel,
        grid=(depth, d // tk),
        in_specs=[pl.BlockSpec((b, tk), lambda l, k: (0, k)),
                  pl.BlockSpec((None, tk, e), lambda l, k: (l, k, 0)),
                  pl.BlockSpec((None, 1, e), lambda l, k: (l, 0, 0))],
        out_specs=pl.BlockSpec((None, b, e), lambda l, k: (l, 0, 0)),
        out_shape=jax.ShapeDtypeStruct((depth, b, e), F32),
        compiler_params=_cparams("parallel", "arbitrary"),
        name="adaln_mod",
    )(c, ada_w, ada_b.reshape(depth, 1, e))


def _rope(y, rc, rs1, rs2):
    reps = y.shape[1] // LANES
    tile = lambda t: jnp.concatenate([t] * reps, axis=1)
    n = y.shape[1]
    half = ROPE_DIM // 2
    return y * tile(rc) + pltpu.roll(y, n - half, 1) * tile(rs1) + pltpu.roll(y, half, 1) * tile(rs2)


def _inproj_kernel(*refs, n_first, per, has_conv, has_dt):
    x_ref, g_ref, sc_ref, sh_ref, w_ref = refs[:5]
    pos = 5
    if has_conv:
        cw_ref, cb_ref = refs[pos:pos + 2]
        pos += 2
    if has_dt:
        wdt_ref = refs[pos]
        pos += 1
    first_ref, second_ref = refs[pos:pos + 2]
    pos += 2
    if has_dt:
        odt_ref = refs[pos]
        pos += 1
    h_scr, perm_scr = refs[pos:pos + 2]
    tm = x_ref.shape[0]
    tn = COL_TILE

    if has_conv:
        halo_scr = refs[pos + 2]

        @pl.when(pl.program_id(0) % per == 0)
        def _():
            halo_scr[...] = jnp.zeros_like(halo_scr)

    h = _norm_mod(x_ref[...], g_ref[...], sc_ref[...], sh_ref[...])
    h_scr[...] = _permute_rows(h, perm_scr, to_time_order=False).astype(BF16)
    if has_dt:
        odt_ref[...] = _dot(h_scr[...], wdt_ref[...])
    for c0 in range(0, w_ref.shape[1], tn):
        acc = _dot(h_scr[...], w_ref[:, c0:c0 + tn])
        cc = slice(c0 - n_first, c0 - n_first + tn)
        if c0 < n_first:
            first_ref[:, c0:c0 + tn] = acc.astype(BF16)
        elif has_conv:
            halo = halo_scr[:, cc]
            for r0 in range(0, tm, PERM_CHUNK):
                x_c = acc[r0:r0 + PERM_CHUNK]
                second_ref[r0:r0 + PERM_CHUNK, cc] = _causal_conv(x_c, halo, cw_ref[:, cc], cb_ref[:, cc]).astype(BF16)
                halo = x_c[PERM_CHUNK - HALO_ROWS:]
            halo_scr[:, cc] = halo
        else:
            second_ref[:, cc] = acc.astype(BF16)


def _inproj(x, seq, g, modl, which, w, n_first, conv_w=None, conv_b=None, w_dt=None):
    t, d = x.shape
    n = w.shape[1]
    n_second = n - n_first
    tm = ROW_TILE
    per = seq // tm
    has_conv = conv_w is not None
    in_specs = [pl.BlockSpec((tm, d), lambda i: (i, 0)),
                pl.BlockSpec((1, d), lambda i: (0, 0)),
                pl.BlockSpec((None, 1, d), lambda i: ((i // per) * N_MOD + which + 1, 0, 0)),
                pl.BlockSpec((None, 1, d), lambda i: ((i // per) * N_MOD + which, 0, 0)),
                _resident((d, n))]
    args = [x, g.reshape(1, d), modl, modl, w]
    scratch = [pltpu.VMEM((tm, d), BF16), pltpu.VMEM((d // LANES, tm, LANES), F32)]
    if has_conv:
        in_specs += [_resident((CONV_TAPS, n_second)), _resident((1, n_second))]
        args += [conv_w, conv_b.reshape(1, n_second)]
        scratch.append(pltpu.VMEM((HALO_ROWS, n_second), F32))
    out_specs = [pl.BlockSpec((tm, n_first), lambda i: (i, 0)), pl.BlockSpec((tm, n_second), lambda i: (i, 0))]
    out_shape = [jax.ShapeDtypeStruct((t, n_first), BF16), jax.ShapeDtypeStruct((t, n_second), BF16)]
    if w_dt is not None:
        in_specs.append(_resident((d, LANES)))
        args.append(w_dt)
        out_specs.append(pl.BlockSpec((tm, LANES), lambda i: (i, 0)))
        out_shape.append(jax.ShapeDtypeStruct((t, LANES), F32))
    return pl.pallas_call(
        functools.partial(_inproj_kernel, n_first=n_first, per=per, has_conv=has_conv, has_dt=w_dt is not None),
        grid=(t // tm,),
        in_specs=in_specs,
        out_specs=out_specs,
        out_shape=out_shape,
        scratch_shapes=scratch,
        compiler_params=_cparams("arbitrary" if has_conv else "parallel"),
        name="inproj",
    )(*args)


def _inproj_qkv_kernel(x_ref, g_ref, sc_ref, sh_ref, w_ref, rc_ref, rs1_ref, rs2_ref, *refs, dilations,
                       q_scale):
    out_refs = refs[:len(dilations)]
    h_scr, tile_scr = refs[len(dilations):]
    tm = x_ref.shape[0]
    tn = COL_TILE

    h_scr[...] = _norm_mod(x_ref[...], g_ref[...], sc_ref[...], sh_ref[...]).astype(BF16)
    for c0 in range(0, w_ref.shape[1], tn):
        acc = _dot(h_scr[...], w_ref[:, c0:c0 + tn])
        grp, kind = divmod(c0 // tn, 3)
        if kind < 2:
            acc = _rope(acc, rc_ref[...], rs1_ref[...], rs2_ref[...])
            if kind == 0:
                acc = acc * q_scale
        o_ref, d = out_refs[grp], dilations[grp]
        if d == 1:
            o_ref[0, :, kind * tn:(kind + 1) * tn] = acc.astype(BF16)
        else:
            for j in range(tn // LANES):
                tile_scr[j] = acc[:, j * LANES:(j + 1) * LANES]
            for r in range(d):
                for j in range(tn // LANES):
                    o_ref[r, :, kind * tn + j * LANES:kind * tn + (j + 1) * LANES] = (
                        tile_scr[j, pl.ds(r, tm // d, stride=d), :].astype(BF16))


def _inproj_qkv(x, seq, g, modl, which, w, rope, dilations, q_scale):
    t, d = x.shape
    n = w.shape[1]
    tm = ROW_TILE
    per = seq // tm
    batch = t // seq
    gw = n // len(dilations)
    in_specs = [pl.BlockSpec((tm, d), lambda i: (i, 0)),
                pl.BlockSpec((1, d), lambda i: (0, 0)),
                pl.BlockSpec((None, 1, d), lambda i: ((i // per) * N_MOD + which + 1, 0, 0)),
                pl.BlockSpec((None, 1, d), lambda i: ((i // per) * N_MOD + which, 0, 0)),
                _resident((d, n))] + [pl.BlockSpec((tm, LANES), lambda i: (i, 0))] * 3
    return pl.pallas_call(
        functools.partial(_inproj_qkv_kernel, dilations=tuple(dilations), q_scale=q_scale),
        grid=(t // tm,),
        in_specs=in_specs,
        out_specs=[pl.BlockSpec((None, dl, None, tm // dl, gw), lambda i: (i // per, 0, i % per, 0, 0))
                   for dl in dilations],
        out_shape=[jax.ShapeDtypeStruct((batch, dl, per, tm // dl, gw), BF16) for dl in dilations],
        scratch_shapes=[pltpu.VMEM((tm, d), BF16), pltpu.VMEM((COL_TILE // LANES, tm, LANES), F32)],
        compiler_params=_cparams("parallel"),
        name="inproj_qkv",
    )(x, g.reshape(1, d), modl, modl, w, *rope)


def _outproj_kernel(y_ref, z_ref, *refs, n_groups):
    if n_groups:
        ng_ref, *refs = refs
    w_ref, x_ref, gate_ref, gp_ref, o_ref, perm_scr = refs
    y = y_ref[...].astype(F32)
    z = z_ref[...].astype(F32)
    if n_groups:
        y = y * _silu_of_twice(z)
        gw = y.shape[1] // n_groups
        y = jnp.concatenate([_rms(y[:, g * gw:(g + 1) * gw]) for g in range(n_groups)], axis=1) * ng_ref[...]
    else:
        y = y * jax.nn.gelu(z, approximate=True)
    out = _permute_rows(_dot(y.astype(BF16), w_ref[...]), perm_scr, to_time_order=True)
    o_ref[...] = _residual(x_ref[...], out, gate_ref[...], gp_ref[...])


def _outproj(y, z, w, x, seq, modl, which, g_post, norm_g=None, n_groups=0):
    t, kdim = y.shape
    d = w.shape[1]
    tm = ROW_TILE
    per = seq // tm
    row = pl.BlockSpec((tm, kdim), lambda i: (i, 0))
    in_specs = [row, row]
    args = [y, z]
    if n_groups:
        in_specs.append(pl.BlockSpec((1, kdim), lambda i: (0, 0)))
        args.append(norm_g.reshape(1, kdim))
    in_specs += [_resident((kdim, d)),
                 pl.BlockSpec((tm, d), lambda i: (i, 0)),
                 pl.BlockSpec((None, 1, d), lambda i: ((i // per) * N_MOD + which, 0, 0)),
                 pl.BlockSpec((1, d), lambda i: (0, 0))]
    return pl.pallas_call(
        functools.partial(_outproj_kernel, n_groups=n_groups),
        grid=(t // tm,),
        in_specs=in_specs,
        out_specs=pl.BlockSpec((tm, d), lambda i: (i, 0)),
        out_shape=jax.ShapeDtypeStruct((t, d), F32),
        scratch_shapes=[pltpu.VMEM((d // LANES, tm, LANES), F32)],
        compiler_params=_cparams("parallel"),
        name="outproj",
    )(*args, w, x, modl, g_post.reshape(1, d))


def _mlp_kernel(x_ref, g_ref, sc_ref, sh_ref, w1_ref, w2_ref, gate_ref, gp_ref, o_ref, h_scr):
    h_scr[...] = _norm_mod(x_ref[...], g_ref[...], sc_ref[...], sh_ref[...]).astype(BF16)
    tf = COL_TILE
    acc = None
    for f0 in range(0, w1_ref.shape[1], tf):
        a = jnp.maximum(_dot(h_scr[...], w1_ref[:, f0:f0 + tf]), 0.0)
        part = _dot((a * a).astype(BF16), w2_ref[f0:f0 + tf, :])
        acc = part if acc is None else acc + part
    o_ref[...] = _residual(x_ref[...], acc, gate_ref[...], gp_ref[...])


def _mlp(x, seq, modl, g_pre, g_post, w1, w2, layer):
    t, d = x.shape
    dff = w1.shape[2]
    tm = ROW_TILE
    per = seq // tm
    mod_spec = lambda which: pl.BlockSpec((None, 1, d), lambda i: ((i // per) * N_MOD + which, 0, 0))
    slab = lambda r, c: pl.BlockSpec((None, r, c), lambda i: (layer, 0, 0), pipeline_mode=pl.Buffered(1))
    return pl.pallas_call(
        _mlp_kernel,
        grid=(t // tm,),
        in_specs=[pl.BlockSpec((tm, d), lambda i: (i, 0)),
                  pl.BlockSpec((1, d), lambda i: (0, 0)),
                  mod_spec(4), mod_spec(3),
                  slab(d, dff), slab(dff, d),
                  mod_spec(5),
                  pl.BlockSpec((1, d), lambda i: (0, 0))],
        out_specs=pl.BlockSpec((tm, d), lambda i: (i, 0)),
        out_shape=jax.ShapeDtypeStruct((t, d), F32),
        scratch_shapes=[pltpu.VMEM((tm, d), BF16)],
        compiler_params=_cparams("parallel"),
        name="mlp",
    )(x, g_pre.reshape(1, d), modl, modl, w1, w2, modl, g_post.reshape(1, d))


def _ssd_decays(dt_raw, dt_bias, a_log_col, n_heads, cum):
    L = dt_raw.shape[0]
    dt_t = _softplus((dt_raw + dt_bias).T[0:n_heads])
    a_col = -jnp.exp(a_log_col[0:n_heads, :])
    a_cs_t = sum(_dot(term, cum) for term in _split3(dt_t * a_col))
    e_cs_t = jnp.exp(a_cs_t)
    dte_t = jnp.exp(a_cs_t[:, L - 1:L] - a_cs_t) * dt_t
    to_cols = lambda v: jnp.concatenate([v, jnp.zeros((LANES - n_heads, L), F32)], axis=0).T
    e_hi, e_mid, _ = _split3(to_cols(e_cs_t))
    d_hi, d_mid, _ = _split3(to_cols(dte_t))
    return to_cols(a_cs_t), a_cs_t - jnp.log(dt_t), e_hi, e_mid, d_hi, d_mid


def _ssd_kernel(xbc_ref, dt_ref, dtn_ref, cum_ref, cw_ref, cb_ref, dtb_ref, alog_ref, dsk_ref, exp_ref, y_ref,
                state_scr, tail_scr, act_scr, *decay_scr, d_inner):
    L, N, G = SSD_CHUNK, SSD_D_STATE, SSD_N_GROUPS
    H = d_inner // SSD_HEAD_DIM
    gw = d_inner // G
    conv_dim = d_inner + 2 * G * N
    c = pl.program_id(1)

    @pl.when(c == 0)
    def _():
        state_scr[...] = jnp.zeros_like(state_scr)
        tail_scr[...] = jnp.zeros_like(tail_scr)
        for scr, v in zip(decay_scr, _ssd_decays(dt_ref[...], dtb_ref[...], alog_ref[...], H, cum_ref[...])):
            scr[...] = v

    a_cs, src_t, e_hi, e_mid, d_hi, d_mid = [scr[...] for scr in decay_scr]
    nxt = _ssd_decays(dtn_ref[...], dtb_ref[...], alog_ref[...], H, cum_ref[...])

    strip = 512
    for s0 in range(0, conv_dim, strip):
        cols = slice(s0, s0 + strip)
        xin = xbc_ref[:, cols].astype(F32)
        act_scr[:, cols] = _silu_of_twice(_causal_conv(xin, tail_scr[:, cols], cw_ref[:, cols], cb_ref[:, cols]))
        tail_scr[:, cols] = xin[L - HALO_ROWS:L]

    row = lax.broadcasted_iota(jnp.int32, (L, L), 0)
    col = lax.broadcasted_iota(jnp.int32, (L, L), 1)
    tri = _perm_time(row) >= _perm_time(col)
    act = lambda lo, hi: act_scr[:, lo:hi]
    lane = lax.broadcasted_iota(jnp.int32, (L, LANES), 1)
    first_head = lane < SSD_HEAD_DIM
    heads_per_pair = LANES // SSD_HEAD_DIM
    for g in range(G):
        gs = slice(g * gw, (g + 1) * gw)
        expand = exp_ref[:, gs]
        e_cs_x = _dot(e_hi, expand) + _dot(e_mid, expand)
        dte_x = _dot(d_hi, expand) + _dot(d_mid, expand)
        xs_g = act(g * gw, (g + 1) * gw)
        bm_g = act(d_inner + g * N, d_inner + (g + 1) * N)
        cm_g = act(d_inner + (G + g) * N, d_inner + (G + g + 1) * N).astype(BF16)
        cb16 = _dot_nt(cm_g, bm_g.astype(BF16)).astype(BF16)
        st_prev = state_scr[g]
        y_off = _dot(cm_g, st_prev.astype(BF16)) * e_cs_x
        w_g = (xs_g * dte_x).astype(BF16)
        new_state = _dot(bm_g.T.astype(BF16), w_g)
        state_scr[g] = st_prev * e_cs_x[L - 1:L, :] + new_state
        y_parts = []
        for q in range(gw // LANES):
            xs_p = xs_g[:, q * LANES:(q + 1) * LANES]
            ms = []
            for hh in range(heads_per_pair):
                h = (g * gw + q * LANES) // SSD_HEAD_DIM + hh
                seg = a_cs[:, h:h + 1] - src_t[h:h + 1, :]
                ms.append(cb16 * jnp.exp(jnp.where(tri, seg, -jnp.inf).astype(BF16)))
            m_cat = jnp.concatenate(ms, axis=1)
            x_cat = jnp.concatenate([jnp.where(first_head, xs_p, 0.0),
                                     jnp.where(first_head, 0.0, xs_p)], axis=0).astype(BF16)
            y_parts.append(_dot(m_cat, x_cat))
        y = jnp.concatenate(y_parts, axis=1) + y_off + xs_g * dsk_ref[:, gs]
        y_ref[:, gs] = y.astype(y_ref.dtype)

    for scr, v in zip(decay_scr, nxt):
        scr[...] = v


def _ssd_core(xbc, dt_raw, batch, seq, conv_w, conv_b, dt_bias, a_log, d_skip):
    t, conv_dim = xbc.shape
    n_heads = dt_bias.shape[0]
    d_inner = n_heads * SSD_HEAD_DIM
    L = SSD_CHUNK
    nc = seq // L
    pad = lambda v: jnp.pad(v.astype(F32), (0, LANES - n_heads))
    expand = (jnp.arange(LANES)[:, None] == (jnp.arange(d_inner)[None, :] // SSD_HEAD_DIM)).astype(BF16)
    dsk_x = jnp.repeat(d_skip.astype(F32), SSD_HEAD_DIM).reshape(1, d_inner)
    when = _perm_time(jnp.arange(L))
    cum = (when[:, None] <= when[None, :]).astype(BF16)
    return pl.pallas_call(
        functools.partial(_ssd_kernel, d_inner=d_inner),
        grid=(batch, nc),
        in_specs=[pl.BlockSpec((L, conv_dim), lambda b, c: (b * nc + c, 0)),
                  pl.BlockSpec((L, LANES), lambda b, c: (b * nc + c, 0)),
                  pl.BlockSpec((L, LANES), lambda b, c: (b * nc + jnp.minimum(c + 1, nc - 1), 0)),
                  _resident((L, L)),
                  _resident((CONV_TAPS, conv_dim)), _resident((1, conv_dim)),
                  _resident((1, LANES)), _resident((LANES, 1)), _resident((1, d_inner)),
                  _resident((LANES, d_inner))],
        out_specs=pl.BlockSpec((L, d_inner), lambda b, c: (b * nc + c, 0)),
        out_shape=jax.ShapeDtypeStruct((t, d_inner), BF16),
        scratch_shapes=[pltpu.VMEM((SSD_N_GROUPS, SSD_D_STATE, d_inner // SSD_N_GROUPS), F32),
                        pltpu.VMEM((HALO_ROWS, conv_dim), F32),
                        pltpu.VMEM((L, conv_dim), F32),
                        pltpu.VMEM((L, LANES), F32), pltpu.VMEM((n_heads, L), F32)]
        + [pltpu.VMEM((L, LANES), BF16)] * 4,
        compiler_params=_cparams("parallel", "arbitrary"),
        name="ssd_core",
    )(xbc, dt_raw, dt_raw, cum, 0.5 * conv_w, 0.5 * conv_b.reshape(1, conv_dim), pad(dt_bias).reshape(1, LANES),
      pad(a_log).reshape(LANES, 1), dsk_x, expand)


def _lru_kernel(xc_ref, wga_ref, bga_ref, wgx_ref, bgx_ref, lam_ref, y_ref, h_scr, a_scr, u_scr):
    tc, width = y_ref.shape
    blk = width // LRU_N_BLOCKS

    @pl.when(pl.program_id(1) == 0)
    def _():
        h_scr[...] = jnp.zeros_like(h_scr)

    neg_sp = _softplus(-lam_ref[...])
    for k in range(LRU_N_BLOCKS):
        sl = slice(k * blk, (k + 1) * blk)
        xb16 = xc_ref[:, sl]
        r = _sigmoid(_dot(xb16, wga_ref[k]) + bga_ref[:, sl])
        i = _sigmoid(_dot(xb16, wgx_ref[k]) + bgx_ref[:, sl])
        log_a = -LRU_C * r * neg_sp[:, sl]
        a_scr[:, sl] = jnp.exp(log_a)
        u_scr[:, sl] = jnp.sqrt(1.0 - jnp.exp(2.0 * log_a)) * (i * xb16.astype(F32))

    row = lax.broadcasted_iota(jnp.int32, (SUBLANES, width), 0)
    h_in = h_scr[...]
    for c0 in range(0, tc, PERM_CHUNK):
        vreg = lambda i: pl.ds(c0 + i * SUBLANES, SUBLANES)
        for i in range(1, PERM_SEG):
            a_i = a_scr[vreg(i), :]
            u_scr[vreg(i), :] = a_i * u_scr[vreg(i - 1), :] + u_scr[vreg(i), :]
            a_scr[vreg(i), :] = a_i * a_scr[vreg(i - 1), :]
        a = a_scr[vreg(PERM_SEG - 1), :]
        u = u_scr[vreg(PERM_SEG - 1), :]
        for k in (1, 2, 4):
            keep = row >= k
            a_sh = jnp.where(keep, pltpu.roll(a, k, 0), 1.0)
            u_sh = jnp.where(keep, pltpu.roll(u, k, 0), 0.0)
            u = a * u_sh + u
            a = a * a_sh
        carry = jnp.where(row == 0, h_in, pltpu.roll(a, 1, 0) * h_in + pltpu.roll(u, 1, 0))
        for i in range(PERM_SEG):
            u_scr[vreg(i), :] = u_scr[vreg(i), :] + a_scr[vreg(i), :] * carry
        last = u_scr[vreg(PERM_SEG - 1), :]
        h_in = jnp.broadcast_to(last[SUBLANES - 1:SUBLANES, :], last.shape)
    h_scr[...] = h_in
    y_ref[...] = u_scr[...].astype(y_ref.dtype)


def _lru_core(xc, batch, seq, w_ga, b_ga, w_gx, b_gx, lam):
    t, width = xc.shape
    tc = LRU_CHUNK
    nc = seq // tc
    vec = lambda v: v.astype(F32).reshape(1, width)
    return pl.pallas_call(
        _lru_kernel,
        grid=(batch, nc),
        in_specs=[pl.BlockSpec((tc, width), lambda b, c: (b * nc + c, 0)),
                  _resident(w_ga.shape), _resident((1, width)), _resident(w_gx.shape),
                  _resident((1, width)), _resident((1, width))],
        out_specs=pl.BlockSpec((tc, width), lambda b, c: (b * nc + c, 0)),
        out_shape=jax.ShapeDtypeStruct((t, width), BF16),
        scratch_shapes=[pltpu.VMEM((SUBLANES, width), F32),
                        pltpu.VMEM((tc, width), F32), pltpu.VMEM((tc, width), F32)],
        compiler_params=_cparams("parallel", "arbitrary"),
        name="lru_core",
    )(xc, w_ga.astype(BF16), vec(b_ga), w_gx.astype(BF16), vec(b_gx), vec(lam))


def _attn_kernel(q_ref, k_ref, v_ref, o_ref, lse_ref, kp_scr, vp_scr, *tile_scr, dilation, span):
    L = span
    width = q_ref.shape[2]
    qblocks = q_ref.shape[1] // L
    n = pl.program_id(1)

    @pl.when(n == 0)
    def _():
        kp_scr[...] = jnp.zeros_like(kp_scr)
        vp_scr[...] = jnp.zeros_like(vp_scr)

    qi = lax.broadcasted_iota(jnp.int32, (L, 2 * L), 0)
    ki = lax.broadcasted_iota(jnp.int32, (L, 2 * L), 1)
    dist = qi + L - ki
    in_band = (dist >= 0) & (dist <= L)
    lane_q = lax.broadcasted_iota(jnp.int32, (L, LANES), 1)
    lane_v = lax.broadcasted_iota(jnp.int32, (2 * L, LANES), 1)
    zero = jnp.zeros((), BF16)
    heads_per_pair = LANES // ATTN_HEAD_DIM

    def block(r, u):
        valid = in_band & (ki + (n * qblocks + u - 1) * L >= 0)
        lse_tile = jnp.zeros((L, LANES), F32)
        o_parts = []
        for j in range(width // LANES):
            sl = slice(j * LANES, (j + 1) * LANES)
            qp = q_ref[r, u * L:(u + 1) * L, sl]
            if u == 0:
                kk = jnp.concatenate([kp_scr[r, :, sl], k_ref[r, 0:L, sl]], axis=0)
                vv = jnp.concatenate([vp_scr[r, :, sl], v_ref[r, 0:L, sl]], axis=0)
            else:
                kk = k_ref[r, (u - 1) * L:(u + 1) * L, sl]
                vv = v_ref[r, (u - 1) * L:(u + 1) * L, sl]
            acc = jnp.zeros((L, LANES), F32)
            for hh in range(heads_per_pair):
                in_head_q = lane_q < ATTN_HEAD_DIM if hh == 0 else lane_q >= ATTN_HEAD_DIM
                in_head_v = lane_v < ATTN_HEAD_DIM if hh == 0 else lane_v >= ATTN_HEAD_DIM
                s = _dot_nt(jnp.where(in_head_q, qp, zero), kk)
                s = jnp.where(valid, s, MASK_VALUE)
                m = jnp.max(s, axis=-1, keepdims=True)
                p = jnp.exp(s - m)
                l = jnp.sum(p, axis=-1, keepdims=True)
                o = _dot(p.astype(BF16), jnp.where(in_head_v, vv, zero))
                acc = acc + o / l
                lse_tile = jnp.where(lane_q == j * heads_per_pair + hh, m + jnp.log(l), lse_tile)
            o_parts.append(acc)
        return o_parts, lse_tile

    def residue(r, carry):
        for u in range(qblocks):
            o_parts, lse_tile = block(r, u)
            if dilation == 1:
                o_ref[u * L:(u + 1) * L, :] = jnp.concatenate(o_parts, axis=1).astype(o_ref.dtype)
                lse_ref[u * L:(u + 1) * L, :] = lse_tile
            else:
                o_scr, lse_scr = tile_scr
                rows = pl.ds(u * L * dilation + r, L, stride=dilation)
                for j, part in enumerate(o_parts):
                    o_scr[j, rows, :] = part
                lse_scr[rows, :] = lse_tile
        kp_scr[r] = k_ref[r, (qblocks - 1) * L:qblocks * L, :]
        vp_scr[r] = v_ref[r, (qblocks - 1) * L:qblocks * L, :]
        return carry

    if dilation == 1:
        residue(0, 0)
    else:
        lax.fori_loop(0, dilation, residue, 0, unroll=2)
        o_scr, lse_scr = tile_scr
        for j in range(width // LANES):
            o_ref[:, j * LANES:(j + 1) * LANES] = o_scr[j].astype(o_ref.dtype)
        lse_ref[...] = lse_scr[...]


def _attn_group(qkv_g, batch, seq, dilation, span):
    width = ATTN_HEADS * ATTN_HEAD_DIM
    m = seq // dilation
    qblocks = 2 if dilation == 1 else 1
    nsteps = m // (span * qblocks)
    t = batch * seq
    rows = span * qblocks * dilation
    spec = lambda which: pl.BlockSpec((None, dilation, span * qblocks, width), lambda b, n: (b, 0, n, which))
    scratch = [pltpu.VMEM((dilation, span, width), BF16), pltpu.VMEM((dilation, span, width), BF16)]
    if dilation > 1:
        scratch += [pltpu.VMEM((width // LANES, rows, LANES), F32), pltpu.VMEM((rows, LANES), F32)]
    return pl.pallas_call(
        functools.partial(_attn_kernel, dilation=dilation, span=span),
        grid=(batch, nsteps),
        in_specs=[spec(0), spec(1), spec(2)],
        out_specs=[pl.BlockSpec((rows, width), lambda b, n: (b * nsteps + n, 0)),
                   pl.BlockSpec((rows, LANES), lambda b, n: (b * nsteps + n, 0))],
        out_shape=[jax.ShapeDtypeStruct((t, width), BF16),
                   jax.ShapeDtypeStruct((t, LANES), F32)],
        scratch_shapes=scratch,
        compiler_params=_cparams("parallel", "arbitrary"),
        name=f"attn_d{dilation}",
    )(qkv_g, qkv_g, qkv_g)


def _attn_out_kernel(o0_ref, o1_ref, o2_ref, l0_ref, l1_ref, l2_ref, w_ref, x_ref, gate_ref, gp_ref,
                     out_ref):
    tm, width = o0_ref.shape
    lane = lax.broadcasted_iota(jnp.int32, (tm, LANES), 1)
    first_head = lane < ATTN_HEAD_DIM
    lses = (l0_ref[...], l1_ref[...], l2_ref[...])
    outs = (o0_ref, o1_ref, o2_ref)
    parts = []
    for j in range(width // LANES):
        sl = slice(j * LANES, (j + 1) * LANES)
        ls = [jnp.where(first_head, l[:, 2 * j:2 * j + 1], l[:, 2 * j + 1:2 * j + 2]) for l in lses]
        mx = jnp.maximum(jnp.maximum(ls[0], ls[1]), ls[2])
        es = [jnp.exp(l - mx) for l in ls]
        num = sum(e * o[:, sl].astype(F32) for e, o in zip(es, outs))
        parts.append(num / (es[0] + es[1] + es[2]))
    o = jnp.concatenate(parts, axis=1).astype(BF16)
    out_ref[...] = _residual(x_ref[...], _dot(o, w_ref[...]), gate_ref[...], gp_ref[...])


def _attn_out(outs, lses, w, x, seq, modl, which, g_post):
    t, d = x.shape
    width = w.shape[0]
    tm = ROW_TILE
    per = seq // tm
    row = lambda n: pl.BlockSpec((tm, n), lambda i: (i, 0))
    return pl.pallas_call(
        _attn_out_kernel,
        grid=(t // tm,),
        in_specs=[row(width)] * 3 + [row(LANES)] * 3 + [
            _resident((width, d)),
            row(d),
            pl.BlockSpec((None, 1, d), lambda i: ((i // per) * N_MOD + which, 0, 0)),
            pl.BlockSpec((1, d), lambda i: (0, 0))],
        out_specs=row(d),
        out_shape=jax.ShapeDtypeStruct((t, d), F32),
        compiler_params=_cparams("parallel"),
        name="attn_out",
    )(*outs, *lses, w, x, modl, g_post.reshape(1, d))


def _rope_tables(positions):
    half = ROPE_DIM // 2
    inv_freq = ROPE_THETA ** (-jnp.arange(0, ROPE_DIM, 2, dtype=F32) / ROPE_DIM)
    ang = positions.astype(F32).reshape(-1, 1) * inv_freq
    cos, sin = jnp.cos(ang), jnp.sin(ang)
    dim = jnp.arange(LANES) % ATTN_HEAD_DIM
    freq = jnp.arange(half)[:, None]
    spread = lambda v, lanes: jnp.dot(v, ((dim[None, :] % half == freq) & lanes[None, :]).astype(F32),
                                      precision=lax.Precision.HIGHEST)
    c = spread(cos, dim < ROPE_DIM) + (dim >= ROPE_DIM).astype(F32)
    s1 = spread(-sin, dim < half)
    s2 = spread(sin, (dim >= half) & (dim < ROPE_DIM))
    return c, s1, s2


def _attention_layer(xf, batch, seq, g_pre, g_post, modl, rope, w_qkv, w_out):
    dilations = [d for _, d in ATTN_CONFIGS]
    qkvs = _inproj_qkv(xf, seq, g_pre, modl, 0, w_qkv, rope, dilations, ATTN_HEAD_DIM ** -0.5)
    outs, lses = [], []
    for qkv_g, (window, dilation) in zip(qkvs, ATTN_CONFIGS):
        qkv_g = qkv_g.reshape(batch, dilation, seq // dilation, qkv_g.shape[-1])
        o, lse = _attn_group(qkv_g, batch, seq, dilation, window // dilation)
        outs.append(o)
        lses.append(lse)
    return _attn_out(outs, lses, w_out, xf, seq, modl, 2, g_post)


def kernel(x, c, positions, ada_w, ada_b, norm_mix_pre, norm_mix_post, norm_mlp_pre, norm_mlp_post, mlp_w1, mlp_w2, ssd_w_in, ssd_conv_w, ssd_conv_b, ssd_dt_bias, ssd_a_log, ssd_d, ssd_norm, ssd_w_out, lru_w_in, lru_conv_w, lru_conv_b, lru_w_gate_a, lru_b_gate_a, lru_w_gate_x, lru_b_gate_x, lru_lambda, lru_w_out, attn_w_qkv, attn_w_out):
    batch, seq, d = x.shape
    depth = ada_w.shape[0]
    xf = x.reshape(batch * seq, d)
    mod = _modulation(c, ada_w, ada_b)
    rope = _rope_tables(positions)
    w1_all, w2_all = mlp_w1.astype(BF16), mlp_w2.astype(BF16)
    for layer in range(depth):
        modl = mod[layer].reshape(batch * N_MOD, 1, d)
        kind, occ = layer % N_MIXERS, layer // N_MIXERS
        if kind == 0:
            w_in = ssd_w_in[occ]
            n_heads = ssd_dt_bias.shape[1]
            d_inner = n_heads * SSD_HEAD_DIM
            n_main = w_in.shape[1] - n_heads
            w_dt = jnp.pad(w_in[:, n_main:], ((0, 0), (0, LANES - n_heads))).astype(BF16)
            z_scale = jnp.where(jnp.arange(n_main) < d_inner, 0.5, 1.0)
            z, xbc, dt_raw = _inproj(xf, seq, norm_mix_pre[layer], modl, 0,
                                     (w_in[:, :n_main] * z_scale).astype(BF16), d_inner, w_dt=w_dt)
            y = _ssd_core(xbc, dt_raw, batch, seq, ssd_conv_w[occ], ssd_conv_b[occ], ssd_dt_bias[occ],
                          ssd_a_log[occ], ssd_d[occ])
            xf = _outproj(y, z, ssd_w_out[occ].astype(BF16), xf, seq, modl, 2, norm_mix_post[layer],
                          norm_g=ssd_norm[occ], n_groups=SSD_N_GROUPS)
        elif kind == 1:
            width = lru_conv_w.shape[-1]
            gate, xc = _inproj(xf, seq, norm_mix_pre[layer], modl, 0, lru_w_in[occ].astype(BF16), width,
                               conv_w=lru_conv_w[occ], conv_b=lru_conv_b[occ])
            hs = _lru_core(xc, batch, seq, lru_w_gate_a[occ], lru_b_gate_a[occ], lru_w_gate_x[occ],
                           lru_b_gate_x[occ], lru_lambda[occ])
            xf = _outproj(hs, gate, lru_w_out[occ].astype(BF16), xf, seq, modl, 2, norm_mix_post[layer])
        else:
            xf = _attention_layer(xf, batch, seq, norm_mix_pre[layer], norm_mix_post[layer], modl, rope,
                                  attn_w_qkv[occ].astype(BF16), attn_w_out[occ].astype(BF16))
        xf = _mlp(xf, seq, modl, norm_mlp_pre[layer], norm_mlp_post[layer], w1_all, w2_all, layer)
    return xf.reshape(batch, seq, d)
```

```python
import functools

import jax
import jax.numpy as jnp
from jax import lax
from jax.experimental import pallas as pl
from jax.experimental.pallas import tpu as pltpu

F32 = jnp.float32
BF16 = jnp.bfloat16

NORM_EPS = 1e-6
N_MIXERS = 3
N_MOD = 6

SSD_HEAD_DIM = 64
SSD_N_GROUPS = 8
SSD_D_STATE = 128
SSD_CHUNK = 128

LRU_N_BLOCKS = 4
LRU_C = 8.0
LRU_CHUNK = 512

ATTN_HEAD_DIM = 64
ATTN_HEADS = 8
ATTN_CONFIGS = ((128, 1), (512, 4), (2048, 16))
ROPE_THETA = 500000.0
ROPE_DIM = ATTN_HEAD_DIM // 4
MASK_VALUE = -1e30

LANES = 128
SUBLANES = 8
VMEM_LIMIT = 56 * 1024 * 1024

ROW_TILE = 1024
COL_TILE = 512


def _cparams(*sem):
    return pltpu.CompilerParams(dimension_semantics=sem, vmem_limit_bytes=VMEM_LIMIT)


def _dot(a, b):
    return jnp.dot(a, b, preferred_element_type=F32)


def _dot_nt(a, b):
    return lax.dot_general(a, b, (((1,), (1,)), ((), ())), preferred_element_type=F32)


def _split3(v):
    hi = v.astype(BF16)
    r1 = v - hi.astype(F32)
    mid = r1.astype(BF16)
    lo = (r1 - mid.astype(F32)).astype(BF16)
    return hi, mid, lo


def _rms(y):
    return y * lax.rsqrt(jnp.mean(y * y, axis=-1, keepdims=True) + NORM_EPS)


def _norm_mod(x, g, sc, sh):
    return _rms(x) * g * (1.0 + sc) + sh


def _residual(x, y, gate, g_post):
    return x + (1.0 + gate) * (_rms(y) * g_post)


def _softplus(x):
    return jnp.maximum(x, 0.0) + jnp.log1p(jnp.exp(-jnp.abs(x)))


def _sigmoid(x):
    return 0.5 + 0.5 * jnp.tanh(0.5 * x)


def _silu_of_twice(h):
    return h + h * jnp.tanh(h)


def _silu(x):
    return _silu_of_twice(0.5 * x)


CONV_TAPS = 4


PERM_CHUNK = SSD_CHUNK
PERM_SEG = PERM_CHUNK // SUBLANES
HALO_ROWS = (CONV_TAPS - 1) * SUBLANES


def _perm_time(r):
    return (r % SUBLANES) * PERM_SEG + r // SUBLANES


def _causal_conv(x, halo, w, b):
    n = PERM_CHUNK
    row8 = lax.broadcasted_iota(jnp.int32, (SUBLANES, x.shape[1]), 0)
    wrapped = []
    for k in range(CONV_TAPS - 1):
        cur = x[n - HALO_ROWS + k * SUBLANES:n - HALO_ROWS + (k + 1) * SUBLANES]
        prev = halo[k * SUBLANES:(k + 1) * SUBLANES]
        wrapped.append(jnp.where(row8 == 0, pltpu.roll(prev, 1, 0), pltpu.roll(cur, 1, 0)))
    acc = x * w[CONV_TAPS - 1:CONV_TAPS, :] + b
    for j in range(1, CONV_TAPS):
        xj = jnp.concatenate(wrapped[CONV_TAPS - 1 - j:] + [x[0:n - j * SUBLANES]], axis=0)
        acc = acc + xj * w[CONV_TAPS - 1 - j:CONV_TAPS - j, :]
    return acc


def _permute_rows(v, scr, to_time_order):
    tm = v.shape[0]
    nb = v.shape[1] // LANES
    for j in range(nb):
        scr[j] = v[:, j * LANES:(j + 1) * LANES]
    cols = []
    for j in range(nb):
        rows = []
        for c0 in range(0, tm, PERM_CHUNK):
            for a in range(PERM_CHUNK // SUBLANES):
                if to_time_order:
                    s, i0 = divmod(a * SUBLANES, PERM_SEG)
                    rows.append(scr[j, pl.ds(c0 + i0 * SUBLANES + s, SUBLANES, stride=SUBLANES), :])
                else:
                    rows.append(scr[j, pl.ds(c0 + a, SUBLANES, stride=PERM_SEG), :])
        cols.append(jnp.concatenate(rows, axis=0))
    return jnp.concatenate(cols, axis=1)


def _resident(shape):
    return pl.BlockSpec(shape, lambda *_: (0,) * len(shape), pipeline_mode=pl.Buffered(1))


def _mod_kernel(c_ref, w_ref, b_ref, o_ref):
    @pl.when(pl.program_id(1) == 0)
    def _():
        o_ref[...] = jnp.broadcast_to(b_ref[...], o_ref.shape)

    a_hi, a_mid, _ = _split3(_silu(c_ref[...]))
    w = w_ref[...]
    w_hi = w.astype(BF16)
    w_lo = (w - w_hi.astype(F32)).astype(BF16)
    o_ref[...] += _dot(a_hi, w_hi) + _dot(a_hi, w_lo) + _dot(a_mid, w_hi)


def _modulation(c, ada_w, ada_b):
    depth, d, e = ada_w.shape
    b = c.shape[0]
    tk = 256
    return pl.pallas_call(
        _mod_kernel,
        grid=(depth, d // tk),
        in_specs=[pl.BlockSpec((b, tk), lambda l, k: (0, k)),
                  pl.BlockSpec((None, tk, e), lambda l, k: (l, k, 0)),
                  pl.BlockSpec((None, 1, e), lambda l, k: (l, 0, 0))],
        out_specs=pl.BlockSpec((None, b, e), lambda l, k: (l, 0, 0)),
        out_shape=jax.ShapeDtypeStruct((depth, b, e), F32),
        compiler_params=_cparams("parallel", "arbitrary"),
        name="adaln_mod",
    )(c, ada_w, ada_b.reshape(depth, 1, e))


def _rope(y, rc, rs1, rs2):
    reps = y.shape[1] // LANES
    tile = lambda t: jnp.concatenate([t] * reps, axis=1)
    n = y.shape[1]
    half = ROPE_DIM // 2
    return y * tile(rc) + pltpu.roll(y, n - half, 1) * tile(rs1) + pltpu.roll(y, half, 1) * tile(rs2)


def _inproj_kernel(*refs, n_first, per, has_conv, has_dt):
    x_ref, g_ref, sc_ref, sh_ref, w_ref = refs[:5]
    pos = 5
    if has_conv:
        cw_ref, cb_ref = refs[pos:pos + 2]
        pos += 2
    if has_dt:
        wdt_ref = refs[pos]
        pos += 1
    first_ref, second_ref = refs[pos:pos + 2]
    pos += 2
    if has_dt:
        odt_ref = refs[pos]
        pos += 1
    h_scr, perm_scr = refs[pos:pos + 2]
    tm = x_ref.shape[0]
    tn = COL_TILE

    if has_conv:
        halo_scr = refs[pos + 2]

        @pl.when(pl.program_id(0) % per == 0)
        def _():
            halo_scr[...] = jnp.zeros_like(halo_scr)

    h = _norm_mod(x_ref[...], g_ref[...], sc_ref[...], sh_ref[...])
    h_scr[...] = _permute_rows(h, perm_scr, to_time_order=False).astype(BF16)
    if has_dt:
        odt_ref[...] = _dot(h_scr[...], wdt_ref[...])
    for c0 in range(0, w_ref.shape[1], tn):
        acc = _dot(h_scr[...], w_ref[:, c0:c0 + tn])
        cc = slice(c0 - n_first, c0 - n_first + tn)
        if c0 < n_first:
            first_ref[:, c0:c0 + tn] = acc.astype(BF16)
        elif has_conv:
            halo = halo_scr[:, cc]
            for r0 in range(0, tm, PERM_CHUNK):
                x_c = acc[r0:r0 + PERM_CHUNK]
                second_ref[r0:r0 + PERM_CHUNK, cc] = _causal_conv(x_c, halo, cw_ref[:, cc], cb_ref[:, cc]).astype(BF16)
                halo = x_c[PERM_CHUNK - HALO_ROWS:]
            halo_scr[:, cc] = halo
        else:
            second_ref[:, cc] = acc.astype(BF16)


def _inproj(x, seq, g, modl, which, w, n_first, conv_w=None, conv_b=None, w_dt=None):
    t, d = x.shape
    n = w.shape[1]
    n_second = n - n_first
    tm = ROW_TILE
    per = seq // tm
    has_conv = conv_w is not None
    in_specs = [pl.BlockSpec((tm, d), lambda i: (i, 0)),
                pl.BlockSpec((1, d), lambda i: (0, 0)),
                pl.BlockSpec((None, 1, d), lambda i: ((i // per) * N_MOD + which + 1, 0, 0)),
                pl.BlockSpec((None, 1, d), lambda i: ((i // per) * N_MOD + which, 0, 0)),
                _resident((d, n))]
    args = [x, g.reshape(1, d), modl, modl, w]
    scratch = [pltpu.VMEM((tm, d), BF16), pltpu.VMEM((d // LANES, tm, LANES), F32)]
    if has_conv:
        in_specs += [_resident((CONV_TAPS, n_second)), _resident((1, n_second))]
        args += [conv_w, conv_b.reshape(1, n_second)]
        scratch.append(pltpu.VMEM((HALO_ROWS, n_second), F32))
    out_specs = [pl.BlockSpec((tm, n_first), lambda i: (i, 0)), pl.BlockSpec((tm, n_second), lambda i: (i, 0))]
    out_shape = [jax.ShapeDtypeStruct((t, n_first), BF16), jax.ShapeDtypeStruct((t, n_second), BF16)]
    if w_dt is not None:
        in_specs.append(_resident((d, LANES)))
        args.append(w_dt)
        out_specs.append(pl.BlockSpec((tm, LANES), lambda i: (i, 0)))
        out_shape.append(jax.ShapeDtypeStruct((t, LANES), F32))
    return pl.pallas_call(
        functools.partial(_inproj_kernel, n_first=n_first, per=per, has_conv=has_conv, has_dt=w_dt is not None),
        grid=(t // tm,),
        in_specs=in_specs,
        out_specs=out_specs,
        out_shape=out_shape,
        scratch_shapes=scratch,
        compiler_params=_cparams("arbitrary" if has_conv else "parallel"),
        name="inproj",
    )(*args)


def _inproj_qkv_kernel(x_ref, g_ref, sc_ref, sh_ref, w_ref, rc_ref, rs1_ref, rs2_ref, *refs, dilations,
                       q_scale):
    out_refs = refs[:len(dilations)]
    h_scr, tile_scr = refs[len(dilations):]
    tm = x_ref.shape[0]
    tn = COL_TILE

    h_scr[...] = _norm_mod(x_ref[...], g_ref[...], sc_ref[...], sh_ref[...]).astype(BF16)
    for c0 in range(0, w_ref.shape[1], tn):
        acc = _dot(h_scr[...], w_ref[:, c0:c0 + tn])
        grp, kind = divmod(c0 // tn, 3)
        if kind < 2:
            acc = _rope(acc, rc_ref[...], rs1_ref[...], rs2_ref[...])
            if kind == 0:
                acc = acc * q_scale
        o_ref, d = out_refs[grp], dilations[grp]
        if d == 1:
            o_ref[0, :, kind * tn:(kind + 1) * tn] = acc.astype(BF16)
        else:
            for j in range(tn // LANES):
                tile_scr[j] = acc[:, j * LANES:(j + 1) * LANES]
            for r in range(d):
                for j in range(tn // LANES):
                    o_ref[r, :, kind * tn + j * LANES:kind * tn + (j + 1) * LANES] = (
                        tile_scr[j, pl.ds(r, tm // d, stride=d), :].astype(BF16))


def _inproj_qkv(x, seq, g, modl, which, w, rope, dilations, q_scale):
    t, d = x.shape
    n = w.shape[1]
    tm = ROW_TILE
    per = seq // tm
    batch = t // seq
    gw = n // len(dilations)
    in_specs = [pl.BlockSpec((tm, d), lambda i: (i, 0)),
                pl.BlockSpec((1, d), lambda i: (0, 0)),
                pl.BlockSpec((None, 1, d), lambda i: ((i // per) * N_MOD + which + 1, 0, 0)),
                pl.BlockSpec((None, 1, d), lambda i: ((i // per) * N_MOD + which, 0, 0)),
                _resident((d, n))] + [pl.BlockSpec((tm, LANES), lambda i: (i, 0))] * 3
    return pl.pallas_call(
        functools.partial(_inproj_qkv_kernel, dilations=tuple(dilations), q_scale=q_scale),
        grid=(t // tm,),
        in_specs=in_specs,
        out_specs=[pl.BlockSpec((None, dl, None, tm // dl, gw), lambda i: (i // per, 0, i % per, 0, 0))
                   for dl in dilations],
        out_shape=[jax.ShapeDtypeStruct((batch, dl, per, tm // dl, gw), BF16) for dl in dilations],
        scratch_shapes=[pltpu.VMEM((tm, d), BF16), pltpu.VMEM((COL_TILE // LANES, tm, LANES), F32)],
        compiler_params=_cparams("parallel"),
        name="inproj_qkv",
    )(x, g.reshape(1, d), modl, modl, w, *rope)


def _outproj_kernel(y_ref, z_ref, *refs, n_groups):
    if n_groups:
        ng_ref, *refs = refs
    w_ref, x_ref, gate_ref, gp_ref, o_ref, perm_scr = refs
    y = y_ref[...].astype(F32)
    z = z_ref[...].astype(F32)
    if n_groups:
        y = y * _silu(z)
        gw = y.shape[1] // n_groups
        y = jnp.concatenate([_rms(y[:, g * gw:(g + 1) * gw]) for g in range(n_groups)], axis=1) * ng_ref[...]
    else:
        y = y * jax.nn.gelu(z, approximate=True)
    out = _permute_rows(_dot(y.astype(BF16), w_ref[...]), perm_scr, to_time_order=True)
    o_ref[...] = _residual(x_ref[...], out, gate_ref[...], gp_ref[...])


def _outproj(y, z, w, x, seq, modl, which, g_post, norm_g=None, n_groups=0):
    t, kdim = y.shape
    d = w.shape[1]
    tm = ROW_TILE
    per = seq // tm
    row = pl.BlockSpec((tm, kdim), lambda i: (i, 0))
    in_specs = [row, row]
    args = [y, z]
    if n_groups:
        in_specs.append(pl.BlockSpec((1, kdim), lambda i: (0, 0)))
        args.append(norm_g.reshape(1, kdim))
    in_specs += [_resident((kdim, d)),
                 pl.BlockSpec((tm, d), lambda i: (i, 0)),
                 pl.BlockSpec((None, 1, d), lambda i: ((i // per) * N_MOD + which, 0, 0)),
                 pl.BlockSpec((1, d), lambda i: (0, 0))]
    return pl.pallas_call(
        functools.partial(_outproj_kernel, n_groups=n_groups),
        grid=(t // tm,),
        in_specs=in_specs,
        out_specs=pl.BlockSpec((tm, d), lambda i: (i, 0)),
        out_shape=jax.ShapeDtypeStruct((t, d), F32),
        scratch_shapes=[pltpu.VMEM((d // LANES, tm, LANES), F32)],
        compiler_params=_cparams("parallel"),
        name="outproj",
    )(*args, w, x, modl, g_post.reshape(1, d))


def _mlp_kernel(x_ref, g_ref, sc_ref, sh_ref, w1_ref, w2_ref, gate_ref, gp_ref, o_ref, h_scr):
    h_scr[...] = _norm_mod(x_ref[...], g_ref[...], sc_ref[...], sh_ref[...]).astype(BF16)
    tf = COL_TILE
    acc = None
    for f0 in range(0, w1_ref.shape[1], tf):
        a = jnp.maximum(_dot(h_scr[...], w1_ref[:, f0:f0 + tf]), 0.0)
        part = _dot((a * a).astype(BF16), w2_ref[f0:f0 + tf, :])
        acc = part if acc is None else acc + part
    o_ref[...] = _residual(x_ref[...], acc, gate_ref[...], gp_ref[...])


def _mlp(x, seq, modl, g_pre, g_post, w1, w2, layer):
    t, d = x.shape
    dff = w1.shape[2]
    tm = ROW_TILE
    per = seq // tm
    mod_spec = lambda which: pl.BlockSpec((None, 1, d), lambda i: ((i // per) * N_MOD + which, 0, 0))
    slab = lambda r, c: pl.BlockSpec((None, r, c), lambda i: (layer, 0, 0), pipeline_mode=pl.Buffered(1))
    return pl.pallas_call(
        _mlp_kernel,
        grid=(t // tm,),
        in_specs=[pl.BlockSpec((tm, d), lambda i: (i, 0)),
                  pl.BlockSpec((1, d), lambda i: (0, 0)),
                  mod_spec(4), mod_spec(3),
                  slab(d, dff), slab(dff, d),
                  mod_spec(5),
                  pl.BlockSpec((1, d), lambda i: (0, 0))],
        out_specs=pl.BlockSpec((tm, d), lambda i: (i, 0)),
        out_shape=jax.ShapeDtypeStruct((t, d), F32),
        scratch_shapes=[pltpu.VMEM((tm, d), BF16)],
        compiler_params=_cparams("parallel"),
        name="mlp",
    )(x, g_pre.reshape(1, d), modl, modl, w1, w2, modl, g_post.reshape(1, d))


def _ssd_decays(dt_raw, dt_bias, a_log_col, n_heads, cum):
    L = dt_raw.shape[0]
    dt_t = _softplus((dt_raw + dt_bias).T[0:n_heads])
    a_col = -jnp.exp(a_log_col[0:n_heads, :])
    a_cs_t = sum(_dot(term, cum) for term in _split3(dt_t * a_col))
    e_cs_t = jnp.exp(a_cs_t)
    dte_t = jnp.exp(a_cs_t[:, L - 1:L] - a_cs_t) * dt_t
    to_cols = lambda v: jnp.concatenate([v, jnp.zeros((LANES - n_heads, L), F32)], axis=0).T
    e_hi, e_mid, _ = _split3(to_cols(e_cs_t))
    d_hi, d_mid, _ = _split3(to_cols(dte_t))
    return to_cols(a_cs_t), a_cs_t - jnp.log(dt_t), e_hi, e_mid, d_hi, d_mid


def _ssd_kernel(xbc_ref, dt_ref, dtn_ref, cum_ref, cw_ref, cb_ref, dtb_ref, alog_ref, dsk_ref, exp_ref, y_ref,
                state_scr, tail_scr, act_scr, *decay_scr, d_inner):
    L, N, G = SSD_CHUNK, SSD_D_STATE, SSD_N_GROUPS
    H = d_inner // SSD_HEAD_DIM
    gw = d_inner // G
    conv_dim = d_inner + 2 * G * N
    c = pl.program_id(1)

    @pl.when(c == 0)
    def _():
        state_scr[...] = jnp.zeros_like(state_scr)
        tail_scr[...] = jnp.zeros_like(tail_scr)
        for scr, v in zip(decay_scr, _ssd_decays(dt_ref[...], dtb_ref[...], alog_ref[...], H, cum_ref[...])):
            scr[...] = v

    a_cs, src_t, e_hi, e_mid, d_hi, d_mid = [scr[...] for scr in decay_scr]
    nxt = _ssd_decays(dtn_ref[...], dtb_ref[...], alog_ref[...], H, cum_ref[...])

    strip = 512
    for s0 in range(0, conv_dim, strip):
        cols = slice(s0, s0 + strip)
        xin = xbc_ref[:, cols].astype(F32)
        act_scr[:, cols] = _silu_of_twice(_causal_conv(xin, tail_scr[:, cols], cw_ref[:, cols], cb_ref[:, cols]))
        tail_scr[:, cols] = xin[L - HALO_ROWS:L]

    row = lax.broadcasted_iota(jnp.int32, (L, L), 0)
    col = lax.broadcasted_iota(jnp.int32, (L, L), 1)
    tri = _perm_time(row) >= _perm_time(col)
    act = lambda lo, hi: act_scr[:, lo:hi]
    lane = lax.broadcasted_iota(jnp.int32, (L, LANES), 1)
    first_head = lane < SSD_HEAD_DIM
    heads_per_pair = LANES // SSD_HEAD_DIM
    for g in range(G):
        gs = slice(g * gw, (g + 1) * gw)
        expand = exp_ref[:, gs]
        e_cs_x = _dot(e_hi, expand) + _dot(e_mid, expand)
        dte_x = _dot(d_hi, expand) + _dot(d_mid, expand)
        xs_g = act(g * gw, (g + 1) * gw)
        bm_g = act(d_inner + g * N, d_inner + (g + 1) * N)
        cm_g = act(d_inner + (G + g) * N, d_inner + (G + g + 1) * N).astype(BF16)
        cb16 = _dot_nt(cm_g, bm_g.astype(BF16)).astype(BF16)
        st_prev = state_scr[g]
        y_off = _dot(cm_g, st_prev.astype(BF16)) * e_cs_x
        w_g = (xs_g * dte_x).astype(BF16)
        new_state = _dot(bm_g.T.astype(BF16), w_g)
        state_scr[g] = st_prev * e_cs_x[L - 1:L, :] + new_state
        y_parts = []
        for q in range(gw // LANES):
            xs_p = xs_g[:, q * LANES:(q + 1) * LANES]
            ms = []
            for hh in range(heads_per_pair):
                h = (g * gw + q * LANES) // SSD_HEAD_DIM + hh
                seg = a_cs[:, h:h + 1] - src_t[h:h + 1, :]
                ms.append(cb16 * jnp.exp(jnp.where(tri, seg, -jnp.inf).astype(BF16)))
            m_cat = jnp.concatenate(ms, axis=1)
            x_cat = jnp.concatenate([jnp.where(first_head, xs_p, 0.0),
                                     jnp.where(first_head, 0.0, xs_p)], axis=0).astype(BF16)
            y_parts.append(_dot(m_cat, x_cat))
        y = jnp.concatenate(y_parts, axis=1) + y_off + xs_g * dsk_ref[:, gs]
        y_ref[:, gs] = y.astype(y_ref.dtype)

    for scr, v in zip(decay_scr, nxt):
        scr[...] = v


def _ssd_core(xbc, dt_raw, batch, seq, conv_w, conv_b, dt_bias, a_log, d_skip):
    t, conv_dim = xbc.shape
    n_heads = dt_bias.shape[0]
    d_inner = n_heads * SSD_HEAD_DIM
    L = SSD_CHUNK
    nc = seq // L
    pad = lambda v: jnp.pad(v.astype(F32), (0, LANES - n_heads))
    expand = (jnp.arange(LANES)[:, None] == (jnp.arange(d_inner)[None, :] // SSD_HEAD_DIM)).astype(BF16)
    dsk_x = jnp.repeat(d_skip.astype(F32), SSD_HEAD_DIM).reshape(1, d_inner)
    when = _perm_time(jnp.arange(L))
    cum = (when[:, None] <= when[None, :]).astype(BF16)
    return pl.pallas_call(
        functools.partial(_ssd_kernel, d_inner=d_inner),
        grid=(batch, nc),
        in_specs=[pl.BlockSpec((L, conv_dim), lambda b, c: (b * nc + c, 0)),
                  pl.BlockSpec((L, LANES), lambda b, c: (b * nc + c, 0)),
                  pl.BlockSpec((L, LANES), lambda b, c: (b * nc + jnp.minimum(c + 1, nc - 1), 0)),
                  _resident((L, L)),
                  _resident((CONV_TAPS, conv_dim)), _resident((1, conv_dim)),
                  _resident((1, LANES)), _resident((LANES, 1)), _resident((1, d_inner)),
                  _resident((LANES, d_inner))],
        out_specs=pl.BlockSpec((L, d_inner), lambda b, c: (b * nc + c, 0)),
        out_shape=jax.ShapeDtypeStruct((t, d_inner), BF16),
        scratch_shapes=[pltpu.VMEM((SSD_N_GROUPS, SSD_D_STATE, d_inner // SSD_N_GROUPS), F32),
                        pltpu.VMEM((HALO_ROWS, conv_dim), F32),
                        pltpu.VMEM((L, conv_dim), F32),
                        pltpu.VMEM((L, LANES), F32), pltpu.VMEM((n_heads, L), F32)]
        + [pltpu.VMEM((L, LANES), BF16)] * 4,
        compiler_params=_cparams("parallel", "arbitrary"),
        name="ssd_core",
    )(xbc, dt_raw, dt_raw, cum, 0.5 * conv_w, 0.5 * conv_b.reshape(1, conv_dim), pad(dt_bias).reshape(1, LANES),
      pad(a_log).reshape(LANES, 1), dsk_x, expand)


def _lru_kernel(xc_ref, wga_ref, bga_ref, wgx_ref, bgx_ref, lam_ref, y_ref, h_scr, a_scr, u_scr):
    tc, width = y_ref.shape
    blk = width // LRU_N_BLOCKS

    @pl.when(pl.program_id(1) == 0)
    def _():
        h_scr[...] = jnp.zeros_like(h_scr)

    neg_sp = _softplus(-lam_ref[...])
    for k in range(LRU_N_BLOCKS):
        sl = slice(k * blk, (k + 1) * blk)
        xb16 = xc_ref[:, sl]
        r = _sigmoid(_dot(xb16, wga_ref[k]) + bga_ref[:, sl])
        i = _sigmoid(_dot(xb16, wgx_ref[k]) + bgx_ref[:, sl])
        log_a = -LRU_C * r * neg_sp[:, sl]
        a_scr[:, sl] = jnp.exp(log_a)
        u_scr[:, sl] = jnp.sqrt(1.0 - jnp.exp(2.0 * log_a)) * (i * xb16.astype(F32))

    row = lax.broadcasted_iota(jnp.int32, (SUBLANES, width), 0)
    h_in = h_scr[...]
    for c0 in range(0, tc, PERM_CHUNK):
        vreg = lambda i: pl.ds(c0 + i * SUBLANES, SUBLANES)
        for i in range(1, PERM_SEG):
            a_i = a_scr[vreg(i), :]
            u_scr[vreg(i), :] = a_i * u_scr[vreg(i - 1), :] + u_scr[vreg(i), :]
            a_scr[vreg(i), :] = a_i * a_scr[vreg(i - 1), :]
        a = a_scr[vreg(PERM_SEG - 1), :]
        u = u_scr[vreg(PERM_SEG - 1), :]
        for k in (1, 2, 4):
            keep = row >= k
            a_sh = jnp.where(keep, pltpu.roll(a, k, 0), 1.0)
            u_sh = jnp.where(keep, pltpu.roll(u, k, 0), 0.0)
            u = a * u_sh + u
            a = a * a_sh
        carry = jnp.where(row == 0, h_in, pltpu.roll(a, 1, 0) * h_in + pltpu.roll(u, 1, 0))
        for i in range(PERM_SEG):
            u_scr[vreg(i), :] = u_scr[vreg(i), :] + a_scr[vreg(i), :] * carry
        last = u_scr[vreg(PERM_SEG - 1), :]
        h_in = jnp.broadcast_to(last[SUBLANES - 1:SUBLANES, :], last.shape)
    h_scr[...] = h_in
    y_ref[...] = u_scr[...].astype(y_ref.dtype)


def _lru_core(xc, batch, seq, w_ga, b_ga, w_gx, b_gx, lam):
    t, width = xc.shape
    tc = LRU_CHUNK
    nc = seq // tc
    vec = lambda v: v.astype(F32).reshape(1, width)
    return pl.pallas_call(
        _lru_kernel,
        grid=(batch, nc),
        in_specs=[pl.BlockSpec((tc, width), lambda b, c: (b * nc + c, 0)),
                  _resident(w_ga.shape), _resident((1, width)), _resident(w_gx.shape),
                  _resident((1, width)), _resident((1, width))],
        out_specs=pl.BlockSpec((tc, width), lambda b, c: (b * nc + c, 0)),
        out_shape=jax.ShapeDtypeStruct((t, width), BF16),
        scratch_shapes=[pltpu.VMEM((SUBLANES, width), F32),
                        pltpu.VMEM((tc, width), F32), pltpu.VMEM((tc, width), F32)],
        compiler_params=_cparams("parallel", "arbitrary"),
        name="lru_core",
    )(xc, w_ga.astype(BF16), vec(b_ga), w_gx.astype(BF16), vec(b_gx), vec(lam))


def _attn_kernel(q_ref, k_ref, v_ref, o_ref, lse_ref, kp_scr, vp_scr, *tile_scr, dilation, span):
    L = span
    width = q_ref.shape[2]
    qblocks = q_ref.shape[1] // L
    n = pl.program_id(1)

    @pl.when(n == 0)
    def _():
        kp_scr[...] = jnp.zeros_like(kp_scr)
        vp_scr[...] = jnp.zeros_like(vp_scr)

    qi = lax.broadcasted_iota(jnp.int32, (L, 2 * L), 0)
    ki = lax.broadcasted_iota(jnp.int32, (L, 2 * L), 1)
    dist = qi + L - ki
    in_band = (dist >= 0) & (dist <= L)
    lane_q = lax.broadcasted_iota(jnp.int32, (L, LANES), 1)
    lane_v = lax.broadcasted_iota(jnp.int32, (2 * L, LANES), 1)
    zero = jnp.zeros((), BF16)
    heads_per_pair = LANES // ATTN_HEAD_DIM

    def block(r, u):
        valid = in_band & (ki + (n * qblocks + u - 1) * L >= 0)
        lse_tile = jnp.zeros((L, LANES), F32)
        o_parts = []
        for j in range(width // LANES):
            sl = slice(j * LANES, (j + 1) * LANES)
            qp = q_ref[r, u * L:(u + 1) * L, sl]
            if u == 0:
                kk = jnp.concatenate([kp_scr[r, :, sl], k_ref[r, 0:L, sl]], axis=0)
                vv = jnp.concatenate([vp_scr[r, :, sl], v_ref[r, 0:L, sl]], axis=0)
            else:
                kk = k_ref[r, (u - 1) * L:(u + 1) * L, sl]
                vv = v_ref[r, (u - 1) * L:(u + 1) * L, sl]
            acc = jnp.zeros((L, LANES), F32)
            for hh in range(heads_per_pair):
                in_head_q = lane_q < ATTN_HEAD_DIM if hh == 0 else lane_q >= ATTN_HEAD_DIM
                in_head_v = lane_v < ATTN_HEAD_DIM if hh == 0 else lane_v >= ATTN_HEAD_DIM
                s = _dot_nt(jnp.where(in_head_q, qp, zero), kk)
                s = jnp.where(valid, s, MASK_VALUE)
                m = jnp.max(s, axis=-1, keepdims=True)
                p = jnp.exp(s - m)
                l = jnp.sum(p, axis=-1, keepdims=True)
                o = _dot(p.astype(BF16), jnp.where(in_head_v, vv, zero))
                acc = acc + o / l
                lse_tile = jnp.where(lane_q == j * heads_per_pair + hh, m + jnp.log(l), lse_tile)
            o_parts.append(acc)
        return o_parts, lse_tile

    def residue(r, carry):
        for u in range(qblocks):
            o_parts, lse_tile = block(r, u)
            if dilation == 1:
                o_ref[u * L:(u + 1) * L, :] = jnp.concatenate(o_parts, axis=1).astype(o_ref.dtype)
                lse_ref[u * L:(u + 1) * L, :] = lse_tile
            else:
                o_scr, lse_scr = tile_scr
                rows = pl.ds(u * L * dilation + r, L, stride=dilation)
                for j, part in enumerate(o_parts):
                    o_scr[j, rows, :] = part
                lse_scr[rows, :] = lse_tile
        kp_scr[r] = k_ref[r, (qblocks - 1) * L:qblocks * L, :]
        vp_scr[r] = v_ref[r, (qblocks - 1) * L:qblocks * L, :]
        return carry

    if dilation == 1:
        residue(0, 0)
    else:
        lax.fori_loop(0, dilation, residue, 0, unroll=2)
        o_scr, lse_scr = tile_scr
        for j in range(width // LANES):
            o_ref[:, j * LANES:(j + 1) * LANES] = o_scr[j].astype(o_ref.dtype)
        lse_ref[...] = lse_scr[...]


def _attn_group(qkv_g, batch, seq, dilation, span):
    width = ATTN_HEADS * ATTN_HEAD_DIM
    m = seq // dilation
    qblocks = 2 if dilation == 1 else 1
    nsteps = m // (span * qblocks)
    t = batch * seq
    rows = span * qblocks * dilation
    spec = lambda which: pl.BlockSpec((None, dilation, span * qblocks, width), lambda b, n: (b, 0, n, which))
    scratch = [pltpu.VMEM((dilation, span, width), BF16), pltpu.VMEM((dilation, span, width), BF16)]
    if dilation > 1:
        scratch += [pltpu.VMEM((width // LANES, rows, LANES), F32), pltpu.VMEM((rows, LANES), F32)]
    return pl.pallas_call(
        functools.partial(_attn_kernel, dilation=dilation, span=span),
        grid=(batch, nsteps),
        in_specs=[spec(0), spec(1), spec(2)],
        out_specs=[pl.BlockSpec((rows, width), lambda b, n: (b * nsteps + n, 0)),
                   pl.BlockSpec((rows, LANES), lambda b, n: (b * nsteps + n, 0))],
        out_shape=[jax.ShapeDtypeStruct((t, width), BF16),
                   jax.ShapeDtypeStruct((t, LANES), F32)],
        scratch_shapes=scratch,
        compiler_params=_cparams("parallel", "arbitrary"),
        name=f"attn_d{dilation}",
    )(qkv_g, qkv_g, qkv_g)


def _attn_out_kernel(o0_ref, o1_ref, o2_ref, l0_ref, l1_ref, l2_ref, w_ref, x_ref, gate_ref, gp_ref,
                     out_ref):
    tm, width = o0_ref.shape
    lane = lax.broadcasted_iota(jnp.int32, (tm, LANES), 1)
    first_head = lane < ATTN_HEAD_DIM
    lses = (l0_ref[...], l1_ref[...], l2_ref[...])
    outs = (o0_ref, o1_ref, o2_ref)
    parts = []
    for j in range(width // LANES):
        sl = slice(j * LANES, (j + 1) * LANES)
        ls = [jnp.where(first_head, l[:, 2 * j:2 * j + 1], l[:, 2 * j + 1:2 * j + 2]) for l in lses]
        mx = jnp.maximum(jnp.maximum(ls[0], ls[1]), ls[2])
        es = [jnp.exp(l - mx) for l in ls]
        num = sum(e * o[:, sl].astype(F32) for e, o in zip(es, outs))
        parts.append(num / (es[0] + es[1] + es[2]))
    o = jnp.concatenate(parts, axis=1).astype(BF16)
    out_ref[...] = _residual(x_ref[...], _dot(o, w_ref[...]), gate_ref[...], gp_ref[...])


def _attn_out(outs, lses, w, x, seq, modl, which, g_post):
    t, d = x.shape
    width = w.shape[0]
    tm = ROW_TILE
    per = seq // tm
    row = lambda n: pl.BlockSpec((tm, n), lambda i: (i, 0))
    return pl.pallas_call(
        _attn_out_kernel,
        grid=(t // tm,),
        in_specs=[row(width)] * 3 + [row(LANES)] * 3 + [
            _resident((width, d)),
            row(d),
            pl.BlockSpec((None, 1, d), lambda i: ((i // per) * N_MOD + which, 0, 0)),
            pl.BlockSpec((1, d), lambda i: (0, 0))],
        out_specs=row(d),
        out_shape=jax.ShapeDtypeStruct((t, d), F32),
        compiler_params=_cparams("parallel"),
        name="attn_out",
    )(*outs, *lses, w, x, modl, g_post.reshape(1, d))


def _rope_tables(positions):
    half = ROPE_DIM // 2
    inv_freq = ROPE_THETA ** (-jnp.arange(0, ROPE_DIM, 2, dtype=F32) / ROPE_DIM)
    ang = positions.astype(F32).reshape(-1, 1) * inv_freq
    cos, sin = jnp.cos(ang), jnp.sin(ang)
    dim = jnp.arange(LANES) % ATTN_HEAD_DIM
    freq = jnp.arange(half)[:, None]
    spread = lambda v, lanes: jnp.dot(v, ((dim[None, :] % half == freq) & lanes[None, :]).astype(F32),
                                      precision=lax.Precision.HIGHEST)
    c = spread(cos, dim < ROPE_DIM) + (dim >= ROPE_DIM).astype(F32)
    s1 = spread(-sin, dim < half)
    s2 = spread(sin, (dim >= half) & (dim < ROPE_DIM))
    return c, s1, s2


def _attention_layer(xf, batch, seq, g_pre, g_post, modl, rope, w_qkv, w_out):
    dilations = [d for _, d in ATTN_CONFIGS]
    qkvs = _inproj_qkv(xf, seq, g_pre, modl, 0, w_qkv, rope, dilations, ATTN_HEAD_DIM ** -0.5)
    outs, lses = [], []
    for qkv_g, (window, dilation) in zip(qkvs, ATTN_CONFIGS):
        qkv_g = qkv_g.reshape(batch, dilation, seq // dilation, qkv_g.shape[-1])
        o, lse = _attn_group(qkv_g, batch, seq, dilation, window // dilation)
        outs.append(o)
        lses.append(lse)
    return _attn_out(outs, lses, w_out, xf, seq, modl, 2, g_post)


def kernel(x, c, positions, ada_w, ada_b, norm_mix_pre, norm_mix_post, norm_mlp_pre, norm_mlp_post, mlp_w1, mlp_w2, ssd_w_in, ssd_conv_w, ssd_conv_b, ssd_dt_bias, ssd_a_log, ssd_d, ssd_norm, ssd_w_out, lru_w_in, lru_conv_w, lru_conv_b, lru_w_gate_a, lru_b_gate_a, lru_w_gate_x, lru_b_gate_x, lru_lambda, lru_w_out, attn_w_qkv, attn_w_out):
    batch, seq, d = x.shape
    depth = ada_w.shape[0]
    xf = x.reshape(batch * seq, d)
    mod = _modulation(c, ada_w, ada_b)
    rope = _rope_tables(positions)
    w1_all, w2_all = mlp_w1.astype(BF16), mlp_w2.astype(BF16)
    for layer in range(depth):
        modl = mod[layer].reshape(batch * N_MOD, 1, d)
        kind, occ = layer % N_MIXERS, layer // N_MIXERS
        if kind == 0:
            w_in = ssd_w_in[occ]
            n_heads = ssd_dt_bias.shape[1]
            d_inner = n_heads * SSD_HEAD_DIM
            n_main = w_in.shape[1] - n_heads
            w_dt = jnp.pad(w_in[:, n_main:], ((0, 0), (0, LANES - n_heads))).astype(BF16)
            z, xbc, dt_raw = _inproj(xf, seq, norm_mix_pre[layer], modl, 0, w_in[:, :n_main].astype(BF16),
                                     d_inner, w_dt=w_dt)
            y = _ssd_core(xbc, dt_raw, batch, seq, ssd_conv_w[occ], ssd_conv_b[occ], ssd_dt_bias[occ],
                          ssd_a_log[occ], ssd_d[occ])
            xf = _outproj(y, z, ssd_w_out[occ].astype(BF16), xf, seq, modl, 2, norm_mix_post[layer],
                          norm_g=ssd_norm[occ], n_groups=SSD_N_GROUPS)
        elif kind == 1:
            width = lru_conv_w.shape[-1]
            gate, xc = _inproj(xf, seq, norm_mix_pre[layer], modl, 0, lru_w_in[occ].astype(BF16), width,
                               conv_w=lru_conv_w[occ], conv_b=lru_conv_b[occ])
            hs = _lru_core(xc, batch, seq, lru_w_gate_a[occ], lru_b_gate_a[occ], lru_w_gate_x[occ],
                           lru_b_gate_x[occ], lru_lambda[occ])
            xf = _outproj(hs, gate, lru_w_out[occ].astype(BF16), xf, seq, modl, 2, norm_mix_post[layer])
        else:
            xf = _attention_layer(xf, batch, seq, norm_mix_pre[layer], norm_mix_post[layer], modl, rope,
                                  attn_w_qkv[occ].astype(BF16), attn_w_out[occ].astype(BF16))
        xf = _mlp(xf, seq, modl, norm_mlp_pre[layer], norm_mlp_post[layer], w1_all, w2_all, layer)
    return xf.reshape(batch, seq, d)
```

```python
import functools

import jax
import jax.numpy as jnp
from jax import lax
from jax.experimental import pallas as pl
from jax.experimental.pallas import tpu as pltpu

F32 = jnp.float32
BF16 = jnp.bfloat16

NORM_EPS = 1e-6
N_MIXERS = 3
N_MOD = 6

SSD_HEAD_DIM = 64
SSD_N_GROUPS = 8
SSD_D_STATE = 128
SSD_CHUNK = 128

LRU_N_BLOCKS = 4
LRU_C = 8.0
LRU_CHUNK = 512

ATTN_HEAD_DIM = 64
ATTN_HEADS = 8
ATTN_CONFIGS = ((128, 1), (512, 4), (2048, 16))
ROPE_THETA = 500000.0
ROPE_DIM = ATTN_HEAD_DIM // 4
MASK_VALUE = -1e30

LANES = 128
SUBLANES = 8
VMEM_LIMIT = 56 * 1024 * 1024

ROW_TILE = 1024
COL_TILE = 512


def _cparams(*sem):
    return pltpu.CompilerParams(dimension_semantics=sem, vmem_limit_bytes=VMEM_LIMIT)


def _dot(a, b):
    return jnp.dot(a, b, preferred_element_type=F32)


def _dot_nt(a, b):
    return lax.dot_general(a, b, (((1,), (1,)), ((), ())), preferred_element_type=F32)


def _split3(v):
    hi = v.astype(BF16)
    r1 = v - hi.astype(F32)
    mid = r1.astype(BF16)
    lo = (r1 - mid.astype(F32)).astype(BF16)
    return hi, mid, lo


def _rms(y):
    return y * lax.rsqrt(jnp.mean(y * y, axis=-1, keepdims=True) + NORM_EPS)


def _norm_mod(x, g, sc, sh):
    return _rms(x) * g * (1.0 + sc) + sh


def _residual(x, y, gate, g_post):
    return x + (1.0 + gate) * (_rms(y) * g_post)


def _softplus(x):
    return jnp.maximum(x, 0.0) + jnp.log1p(jnp.exp(-jnp.abs(x)))


def _sigmoid(x):
    return 0.5 + 0.5 * jnp.tanh(0.5 * x)


def _silu_of_twice(h):
    return h + h * jnp.tanh(h)


def _silu(x):
    return _silu_of_twice(0.5 * x)


CONV_TAPS = 4


PERM_CHUNK = SSD_CHUNK
PERM_SEG = PERM_CHUNK // SUBLANES
HALO_ROWS = (CONV_TAPS - 1) * SUBLANES


def _perm_time(r):
    return (r % SUBLANES) * PERM_SEG + r // SUBLANES


def _causal_conv(x, halo, w, b):
    n = PERM_CHUNK
    row8 = lax.broadcasted_iota(jnp.int32, (SUBLANES, x.shape[1]), 0)
    wrapped = []
    for k in range(CONV_TAPS - 1):
        cur = x[n - HALO_ROWS + k * SUBLANES:n - HALO_ROWS + (k + 1) * SUBLANES]
        prev = halo[k * SUBLANES:(k + 1) * SUBLANES]
        wrapped.append(jnp.where(row8 == 0, pltpu.roll(prev, 1, 0), pltpu.roll(cur, 1, 0)))
    acc = x * w[CONV_TAPS - 1:CONV_TAPS, :] + b
    for j in range(1, CONV_TAPS):
        xj = jnp.concatenate(wrapped[CONV_TAPS - 1 - j:] + [x[0:n - j * SUBLANES]], axis=0)
        acc = acc + xj * w[CONV_TAPS - 1 - j:CONV_TAPS - j, :]
    return acc


def _permute_rows(v, scr, to_time_order):
    tm = v.shape[0]
    nb = v.shape[1] // LANES
    for j in range(nb):
        scr[j] = v[:, j * LANES:(j + 1) * LANES]
    cols = []
    for j in range(nb):
        rows = []
        for c0 in range(0, tm, PERM_CHUNK):
            for a in range(PERM_CHUNK // SUBLANES):
                if to_time_order:
                    s, i0 = divmod(a * SUBLANES, PERM_SEG)
                    rows.append(scr[j, pl.ds(c0 + i0 * SUBLANES + s, SUBLANES, stride=SUBLANES), :])
                else:
                    rows.append(scr[j, pl.ds(c0 + a, SUBLANES, stride=PERM_SEG), :])
        cols.append(jnp.concatenate(rows, axis=0))
    return jnp.concatenate(cols, axis=1)


def _resident(shape):
    return pl.BlockSpec(shape, lambda *_: (0,) * len(shape), pipeline_mode=pl.Buffered(1))


def _mod_kernel(c_ref, w_ref, b_ref, o_ref):
    @pl.when(pl.program_id(1) == 0)
    def _():
        o_ref[...] = jnp.broadcast_to(b_ref[...], o_ref.shape)

    a_hi, a_mid, _ = _split3(_silu(c_ref[...]))
    w = w_ref[...]
    w_hi = w.astype(BF16)
    w_lo = (w - w_hi.astype(F32)).astype(BF16)
    o_ref[...] += _dot(a_hi, w_hi) + _dot(a_hi, w_lo) + _dot(a_mid, w_hi)


def _modulation(c, ada_w, ada_b):
    depth, d, e = ada_w.shape
    b = c.shape[0]
    tk = 256
    return pl.pallas_call(
        _mod_kernel,
        grid=(depth, d // tk),
        in_specs=[pl.BlockSpec((b, tk), lambda l, k: (0, k)),
                  pl.BlockSpec((None, tk, e), lambda l, k: (l, k, 0)),
                  pl.BlockSpec((None, 1, e), lambda l, k: (l, 0, 0))],
        out_specs=pl.BlockSpec((None, b, e), lambda l, k: (l, 0, 0)),
        out_shape=jax.ShapeDtypeStruct((depth, b, e), F32),
        compiler_params=_cparams("parallel", "arbitrary"),
        name="adaln_mod",
    )(c, ada_w, ada_b.reshape(depth, 1, e))


def _rope(y, rc, rs1, rs2):
    reps = y.shape[1] // LANES
    tile = lambda t: jnp.concatenate([t] * reps, axis=1)
    n = y.shape[1]
    half = ROPE_DIM // 2
    return y * tile(rc) + pltpu.roll(y, n - half, 1) * tile(rs1) + pltpu.roll(y, half, 1) * tile(rs2)


def _inproj_kernel(*refs, n_first, per, has_conv, has_dt):
    x_ref, g_ref, sc_ref, sh_ref, w_ref = refs[:5]
    pos = 5
    if has_conv:
        cw_ref, cb_ref = refs[pos:pos + 2]
        pos += 2
    if has_dt:
        wdt_ref = refs[pos]
        pos += 1
    first_ref, second_ref = refs[pos:pos + 2]
    pos += 2
    if has_dt:
        odt_ref = refs[pos]
        pos += 1
    h_scr, perm_scr = refs[pos:pos + 2]
    tm = x_ref.shape[0]
    tn = COL_TILE

    if has_conv:
        halo_scr = refs[pos + 2]

        @pl.when(pl.program_id(0) % per == 0)
        def _():
            halo_scr[...] = jnp.zeros_like(halo_scr)

    h = _norm_mod(x_ref[...], g_ref[...], sc_ref[...], sh_ref[...])
    h_scr[...] = _permute_rows(h, perm_scr, to_time_order=False).astype(BF16)
    if has_dt:
        odt_ref[...] = _dot(h_scr[...], wdt_ref[...])
    for c0 in range(0, w_ref.shape[1], tn):
        acc = _dot(h_scr[...], w_ref[:, c0:c0 + tn])
        cc = slice(c0 - n_first, c0 - n_first + tn)
        if c0 < n_first:
            first_ref[:, c0:c0 + tn] = acc.astype(BF16)
        elif has_conv:
            halo = halo_scr[:, cc]
            for r0 in range(0, tm, PERM_CHUNK):
                x_c = acc[r0:r0 + PERM_CHUNK]
                second_ref[r0:r0 + PERM_CHUNK, cc] = _causal_conv(x_c, halo, cw_ref[:, cc], cb_ref[:, cc]).astype(BF16)
                halo = x_c[PERM_CHUNK - HALO_ROWS:]
            halo_scr[:, cc] = halo
        else:
            second_ref[:, cc] = acc.astype(BF16)


def _inproj(x, seq, g, modl, which, w, n_first, conv_w=None, conv_b=None, w_dt=None):
    t, d = x.shape
    n = w.shape[1]
    n_second = n - n_first
    tm = ROW_TILE
    per = seq // tm
    has_conv = conv_w is not None
    in_specs = [pl.BlockSpec((tm, d), lambda i: (i, 0)),
                pl.BlockSpec((1, d), lambda i: (0, 0)),
                pl.BlockSpec((None, 1, d), lambda i: ((i // per) * N_MOD + which + 1, 0, 0)),
                pl.BlockSpec((None, 1, d), lambda i: ((i // per) * N_MOD + which, 0, 0)),
                _resident((d, n))]
    args = [x, g.reshape(1, d), modl, modl, w]
    scratch = [pltpu.VMEM((tm, d), BF16), pltpu.VMEM((d // LANES, tm, LANES), F32)]
    if has_conv:
        in_specs += [_resident((CONV_TAPS, n_second)), _resident((1, n_second))]
        args += [conv_w, conv_b.reshape(1, n_second)]
        scratch.append(pltpu.VMEM((HALO_ROWS, n_second), F32))
    out_specs = [pl.BlockSpec((tm, n_first), lambda i: (i, 0)), pl.BlockSpec((tm, n_second), lambda i: (i, 0))]
    out_shape = [jax.ShapeDtypeStruct((t, n_first), BF16), jax.ShapeDtypeStruct((t, n_second), BF16)]
    if w_dt is not None:
        in_specs.append(_resident((d, LANES)))
        args.append(w_dt)
        out_specs.append(pl.BlockSpec((tm, LANES), lambda i: (i, 0)))
        out_shape.append(jax.ShapeDtypeStruct((t, LANES), F32))
    return pl.pallas_call(
        functools.partial(_inproj_kernel, n_first=n_first, per=per, has_conv=has_conv, has_dt=w_dt is not None),
        grid=(t // tm,),
        in_specs=in_specs,
        out_specs=out_specs,
        out_shape=out_shape,
        scratch_shapes=scratch,
        compiler_params=_cparams("arbitrary" if has_conv else "parallel"),
        name="inproj",
    )(*args)


def _inproj_qkv_kernel(x_ref, g_ref, sc_ref, sh_ref, w_ref, rc_ref, rs1_ref, rs2_ref, *refs, dilations,
                       q_scale):
    out_refs = refs[:len(dilations)]
    h_scr, tile_scr = refs[len(dilations):]
    tm = x_ref.shape[0]
    tn = COL_TILE

    h_scr[...] = _norm_mod(x_ref[...], g_ref[...], sc_ref[...], sh_ref[...]).astype(BF16)
    for c0 in range(0, w_ref.shape[1], tn):
        acc = _dot(h_scr[...], w_ref[:, c0:c0 + tn])
        grp, kind = divmod(c0 // tn, 3)
        if kind < 2:
            acc = _rope(acc, rc_ref[...], rs1_ref[...], rs2_ref[...])
            if kind == 0:
                acc = acc * q_scale
        o_ref, d = out_refs[grp], dilations[grp]
        if d == 1:
            o_ref[0, :, kind * tn:(kind + 1) * tn] = acc.astype(BF16)
        else:
            for j in range(tn // LANES):
                tile_scr[j] = acc[:, j * LANES:(j + 1) * LANES]
            for r in range(d):
                for j in range(tn // LANES):
                    o_ref[r, :, kind * tn + j * LANES:kind * tn + (j + 1) * LANES] = (
                        tile_scr[j, pl.ds(r, tm // d, stride=d), :].astype(BF16))


def _inproj_qkv(x, seq, g, modl, which, w, rope, dilations, q_scale):
    t, d = x.shape
    n = w.shape[1]
    tm = ROW_TILE
    per = seq // tm
    batch = t // seq
    gw = n // len(dilations)
    in_specs = [pl.BlockSpec((tm, d), lambda i: (i, 0)),
                pl.BlockSpec((1, d), lambda i: (0, 0)),
                pl.BlockSpec((None, 1, d), lambda i: ((i // per) * N_MOD + which + 1, 0, 0)),
                pl.BlockSpec((None, 1, d), lambda i: ((i // per) * N_MOD + which, 0, 0)),
                _resident((d, n))] + [pl.BlockSpec((tm, LANES), lambda i: (i, 0))] * 3
    return pl.pallas_call(
        functools.partial(_inproj_qkv_kernel, dilations=tuple(dilations), q_scale=q_scale),
        grid=(t // tm,),
        in_specs=in_specs,
        out_specs=[pl.BlockSpec((None, dl, None, tm // dl, gw), lambda i: (i // per, 0, i % per, 0, 0))
                   for dl in dilations],
        out_shape=[jax.ShapeDtypeStruct((batch, dl, per, tm // dl, gw), BF16) for dl in dilations],
        scratch_shapes=[pltpu.VMEM((tm, d), BF16), pltpu.VMEM((COL_TILE // LANES, tm, LANES), F32)],
        compiler_params=_cparams("parallel"),
        name="inproj_qkv",
    )(x, g.reshape(1, d), modl, modl, w, *rope)


def _outproj_kernel(y_ref, z_ref, *refs, n_groups):
    if n_groups:
        ng_ref, *refs = refs
    w_ref, x_ref, gate_ref, gp_ref, o_ref, perm_scr = refs
    y = y_ref[...].astype(F32)
    z = z_ref[...].astype(F32)
    if n_groups:
        y = y * _silu(z)
        gw = y.shape[1] // n_groups
        y = jnp.concatenate([_rms(y[:, g * gw:(g + 1) * gw]) for g in range(n_groups)], axis=1) * ng_ref[...]
    else:
        y = y * jax.nn.gelu(z, approximate=True)
    out = _permute_rows(_dot(y.astype(BF16), w_ref[...]), perm_scr, to_time_order=True)
    o_ref[...] = _residual(x_ref[...], out, gate_ref[...], gp_ref[...])


def _outproj(y, z, w, x, seq, modl, which, g_post, norm_g=None, n_groups=0):
    t, kdim = y.shape
    d = w.shape[1]
    tm = ROW_TILE
    per = seq // tm
    row = pl.BlockSpec((tm, kdim), lambda i: (i, 0))
    in_specs = [row, row]
    args = [y, z]
    if n_groups:
        in_specs.append(pl.BlockSpec((1, kdim), lambda i: (0, 0)))
        args.append(norm_g.reshape(1, kdim))
    in_specs += [_resident((kdim, d)),
                 pl.BlockSpec((tm, d), lambda i: (i, 0)),
                 pl.BlockSpec((None, 1, d), lambda i: ((i // per) * N_MOD + which, 0, 0)),
                 pl.BlockSpec((1, d), lambda i: (0, 0))]
    return pl.pallas_call(
        functools.partial(_outproj_kernel, n_groups=n_groups),
        grid=(t // tm,),
        in_specs=in_specs,
        out_specs=pl.BlockSpec((tm, d), lambda i: (i, 0)),
        out_shape=jax.ShapeDtypeStruct((t, d), F32),
        scratch_shapes=[pltpu.VMEM((d // LANES, tm, LANES), F32)],
        compiler_params=_cparams("parallel"),
        name="outproj",
    )(*args, w, x, modl, g_post.reshape(1, d))


def _mlp_kernel(x_ref, g_ref, sc_ref, sh_ref, w1_ref, w2_ref, gate_ref, gp_ref, o_ref, h_scr):
    h_scr[...] = _norm_mod(x_ref[...], g_ref[...], sc_ref[...], sh_ref[...]).astype(BF16)
    tf = COL_TILE
    acc = None
    for f0 in range(0, w1_ref.shape[1], tf):
        a = jnp.maximum(_dot(h_scr[...], w1_ref[:, f0:f0 + tf]), 0.0)
        part = _dot((a * a).astype(BF16), w2_ref[f0:f0 + tf, :])
        acc = part if acc is None else acc + part
    o_ref[...] = _residual(x_ref[...], acc, gate_ref[...], gp_ref[...])


def _mlp(x, seq, modl, g_pre, g_post, w1, w2, layer):
    t, d = x.shape
    dff = w1.shape[2]
    tm = ROW_TILE
    per = seq // tm
    mod_spec = lambda which: pl.BlockSpec((None, 1, d), lambda i: ((i // per) * N_MOD + which, 0, 0))
    slab = lambda r, c: pl.BlockSpec((None, r, c), lambda i: (layer, 0, 0), pipeline_mode=pl.Buffered(1))
    return pl.pallas_call(
        _mlp_kernel,
        grid=(t // tm,),
        in_specs=[pl.BlockSpec((tm, d), lambda i: (i, 0)),
                  pl.BlockSpec((1, d), lambda i: (0, 0)),
                  mod_spec(4), mod_spec(3),
                  slab(d, dff), slab(dff, d),
                  mod_spec(5),
                  pl.BlockSpec((1, d), lambda i: (0, 0))],
        out_specs=pl.BlockSpec((tm, d), lambda i: (i, 0)),
        out_shape=jax.ShapeDtypeStruct((t, d), F32),
        scratch_shapes=[pltpu.VMEM((tm, d), BF16)],
        compiler_params=_cparams("parallel"),
        name="mlp",
    )(x, g_pre.reshape(1, d), modl, modl, w1, w2, modl, g_post.reshape(1, d))


def _ssd_decays(dt_raw, dt_bias, a_log_col, n_heads, cum):
    L = dt_raw.shape[0]
    dt_t = _softplus((dt_raw + dt_bias).T[0:n_heads])
    a_col = -jnp.exp(a_log_col[0:n_heads, :])
    a_cs_t = sum(_dot(term, cum) for term in _split3(dt_t * a_col))
    e_cs_t = jnp.exp(a_cs_t)
    dte_t = jnp.exp(a_cs_t[:, L - 1:L] - a_cs_t) * dt_t
    to_cols = lambda v: jnp.concatenate([v, jnp.zeros((LANES - n_heads, L), F32)], axis=0).T
    e_hi, e_mid, _ = _split3(to_cols(e_cs_t))
    d_hi, d_mid, _ = _split3(to_cols(dte_t))
    return to_cols(a_cs_t), a_cs_t - jnp.log(dt_t), e_hi, e_mid, d_hi, d_mid


def _ssd_kernel(xbc_ref, dt_ref, dtn_ref, cum_ref, cw_ref, cb_ref, dtb_ref, alog_ref, dsk_ref, exp_ref, y_ref,
                state_scr, tail_scr, act_scr, *decay_scr, d_inner):
    L, N, G = SSD_CHUNK, SSD_D_STATE, SSD_N_GROUPS
    H = d_inner // SSD_HEAD_DIM
    gw = d_inner // G
    conv_dim = d_inner + 2 * G * N
    c = pl.program_id(1)

    @pl.when(c == 0)
    def _():
        state_scr[...] = jnp.zeros_like(state_scr)
        tail_scr[...] = jnp.zeros_like(tail_scr)
        for scr, v in zip(decay_scr, _ssd_decays(dt_ref[...], dtb_ref[...], alog_ref[...], H, cum_ref[...])):
            scr[...] = v

    a_cs, src_t, e_hi, e_mid, d_hi, d_mid = [scr[...] for scr in decay_scr]
    nxt = _ssd_decays(dtn_ref[...], dtb_ref[...], alog_ref[...], H, cum_ref[...])

    strip = 512
    for s0 in range(0, conv_dim, strip):
        cols = slice(s0, s0 + strip)
        xin = xbc_ref[:, cols].astype(F32)
        act_scr[:, cols] = _silu_of_twice(_causal_conv(xin, tail_scr[:, cols], cw_ref[:, cols], cb_ref[:, cols]))
        tail_scr[:, cols] = xin[L - HALO_ROWS:L]

    row = lax.broadcasted_iota(jnp.int32, (L, L), 0)
    col = lax.broadcasted_iota(jnp.int32, (L, L), 1)
    tri = _perm_time(row) >= _perm_time(col)
    act = lambda lo, hi: act_scr[:, lo:hi]
    lane = lax.broadcasted_iota(jnp.int32, (L, LANES), 1)
    first_head = lane < SSD_HEAD_DIM
    heads_per_pair = LANES // SSD_HEAD_DIM
    for g in range(G):
        gs = slice(g * gw, (g + 1) * gw)
        expand = exp_ref[:, gs]
        e_cs_x = _dot(e_hi, expand) + _dot(e_mid, expand)
        dte_x = _dot(d_hi, expand) + _dot(d_mid, expand)
        xs_g = act(g * gw, (g + 1) * gw)
        bm_g = act(d_inner + g * N, d_inner + (g + 1) * N)
        cm_g = act(d_inner + (G + g) * N, d_inner + (G + g + 1) * N).astype(BF16)
        cb16 = _dot_nt(cm_g, bm_g.astype(BF16)).astype(BF16)
        st_prev = state_scr[g]
        y_off = _dot(cm_g, st_prev.astype(BF16)) * e_cs_x
        w_g = (xs_g * dte_x).astype(BF16)
        new_state = _dot(bm_g.T.astype(BF16), w_g)
        state_scr[g] = st_prev * e_cs_x[L - 1:L, :] + new_state
        y_parts = []
        for q in range(gw // LANES):
            xs_p = xs_g[:, q * LANES:(q + 1) * LANES]
            ms = []
            for hh in range(heads_per_pair):
                h = (g * gw + q * LANES) // SSD_HEAD_DIM + hh
                seg = a_cs[:, h:h + 1] - src_t[h:h + 1, :]
                ms.append(cb16 * jnp.exp(jnp.where(tri, seg, -jnp.inf).astype(BF16)))
            m_cat = jnp.concatenate(ms, axis=1)
            x_cat = jnp.concatenate([jnp.where(first_head, xs_p, 0.0),
                                     jnp.where(first_head, 0.0, xs_p)], axis=0).astype(BF16)
            y_parts.append(_dot(m_cat, x_cat))
        y = jnp.concatenate(y_parts, axis=1) + y_off + xs_g * dsk_ref[:, gs]
        y_ref[:, gs] = y.astype(y_ref.dtype)

    for scr, v in zip(decay_scr, nxt):
        scr[...] = v


def _ssd_core(xbc, dt_raw, batch, seq, conv_w, conv_b, dt_bias, a_log, d_skip):
    t, conv_dim = xbc.shape
    n_heads = dt_bias.shape[0]
    d_inner = n_heads * SSD_HEAD_DIM
    L = SSD_CHUNK
    nc = seq // L
    pad = lambda v: jnp.pad(v.astype(F32), (0, LANES - n_heads))
    expand = (jnp.arange(LANES)[:, None] == (jnp.arange(d_inner)[None, :] // SSD_HEAD_DIM)).astype(BF16)
    dsk_x = jnp.repeat(d_skip.astype(F32), SSD_HEAD_DIM).reshape(1, d_inner)
    when = _perm_time(jnp.arange(L))
    cum = (when[:, None] <= when[None, :]).astype(BF16)
    return pl.pallas_call(
        functools.partial(_ssd_kernel, d_inner=d_inner),
        grid=(batch, nc),
        in_specs=[pl.BlockSpec((L, conv_dim), lambda b, c: (b * nc + c, 0)),
                  pl.BlockSpec((L, LANES), lambda b, c: (b * nc + c, 0)),
                  pl.BlockSpec((L, LANES), lambda b, c: (b * nc + jnp.minimum(c + 1, nc - 1), 0)),
                  _resident((L, L)),
                  _resident((CONV_TAPS, conv_dim)), _resident((1, conv_dim)),
                  _resident((1, LANES)), _resident((LANES, 1)), _resident((1, d_inner)),
                  _resident((LANES, d_inner))],
        out_specs=pl.BlockSpec((L, d_inner), lambda b, c: (b * nc + c, 0)),
        out_shape=jax.ShapeDtypeStruct((t, d_inner), BF16),
        scratch_shapes=[pltpu.VMEM((SSD_N_GROUPS, SSD_D_STATE, d_inner // SSD_N_GROUPS), F32),
                        pltpu.VMEM((HALO_ROWS, conv_dim), F32),
                        pltpu.VMEM((L, conv_dim), F32),
                        pltpu.VMEM((L, LANES), F32), pltpu.VMEM((n_heads, L), F32)]
        + [pltpu.VMEM((L, LANES), BF16)] * 4,
        compiler_params=_cparams("parallel", "arbitrary"),
        name="ssd_core",
    )(xbc, dt_raw, dt_raw, cum, 0.5 * conv_w, 0.5 * conv_b.reshape(1, conv_dim), pad(dt_bias).reshape(1, LANES),
      pad(a_log).reshape(LANES, 1), dsk_x, expand)


def _lru_kernel(xc_ref, wga_ref, bga_ref, wgx_ref, bgx_ref, lam_ref, y_ref, h_scr, a_scr, u_scr):
    tc, width = y_ref.shape
    blk = width // LRU_N_BLOCKS

    @pl.when(pl.program_id(1) == 0)
    def _():
        h_scr[...] = jnp.zeros_like(h_scr)

    neg_sp = _softplus(-lam_ref[...])
    for k in range(LRU_N_BLOCKS):
        sl = slice(k * blk, (k + 1) * blk)
        xb16 = xc_ref[:, sl]
        r = _sigmoid(_dot(xb16, wga_ref[k]) + bga_ref[:, sl])
        i = _sigmoid(_dot(xb16, wgx_ref[k]) + bgx_ref[:, sl])
        log_a = -LRU_C * r * neg_sp[:, sl]
        a_scr[:, sl] = jnp.exp(log_a)
        u_scr[:, sl] = jnp.sqrt(1.0 - jnp.exp(2.0 * log_a)) * (i * xb16.astype(F32))

    row = lax.broadcasted_iota(jnp.int32, (SUBLANES, width), 0)
    h_in = h_scr[...]
    for c0 in range(0, tc, PERM_CHUNK):
        vreg = lambda i: pl.ds(c0 + i * SUBLANES, SUBLANES)
        for i in range(1, PERM_SEG):
            a_i = a_scr[vreg(i), :]
            u_scr[vreg(i), :] = a_i * u_scr[vreg(i - 1), :] + u_scr[vreg(i), :]
            a_scr[vreg(i), :] = a_i * a_scr[vreg(i - 1), :]
        a = a_scr[vreg(PERM_SEG - 1), :]
        u = u_scr[vreg(PERM_SEG - 1), :]
        for k in (1, 2, 4):
            keep = row >= k
            a_sh = jnp.where(keep, pltpu.roll(a, k, 0), 1.0)
            u_sh = jnp.where(keep, pltpu.roll(u, k, 0), 0.0)
            u = a * u_sh + u
            a = a * a_sh
        carry = jnp.where(row == 0, h_in, pltpu.roll(a, 1, 0) * h_in + pltpu.roll(u, 1, 0))
        for i in range(PERM_SEG):
            u_scr[vreg(i), :] = u_scr[vreg(i), :] + a_scr[vreg(i), :] * carry
        last = u_scr[vreg(PERM_SEG - 1), :]
        h_in = jnp.broadcast_to(last[SUBLANES - 1:SUBLANES, :], last.shape)
    h_scr[...] = h_in
    y_ref[...] = u_scr[...].astype(y_ref.dtype)


def _lru_core(xc, batch, seq, w_ga, b_ga, w_gx, b_gx, lam):
    t, width = xc.shape
    tc = LRU_CHUNK
    nc = seq // tc
    vec = lambda v: v.astype(F32).reshape(1, width)
    return pl.pallas_call(
        _lru_kernel,
        grid=(batch, nc),
        in_specs=[pl.BlockSpec((tc, width), lambda b, c: (b * nc + c, 0)),
                  _resident(w_ga.shape), _resident((1, width)), _resident(w_gx.shape),
                  _resident((1, width)), _resident((1, width))],
        out_specs=pl.BlockSpec((tc, width), lambda b, c: (b * nc + c, 0)),
        out_shape=jax.ShapeDtypeStruct((t, width), BF16),
        scratch_shapes=[pltpu.VMEM((SUBLANES, width), F32),
                        pltpu.VMEM((tc, width), F32), pltpu.VMEM((tc, width), F32)],
        compiler_params=_cparams("parallel", "arbitrary"),
        name="lru_core",
    )(xc, w_ga.astype(BF16), vec(b_ga), w_gx.astype(BF16), vec(b_gx), vec(lam))


def _attn_kernel(q_ref, k_ref, v_ref, o_ref, lse_ref, kp_scr, vp_scr, *tile_scr, dilation, span):
    L = span
    width = q_ref.shape[2]
    qblocks = q_ref.shape[1] // L
    n = pl.program_id(1)

    @pl.when(n == 0)
    def _():
        kp_scr[...] = jnp.zeros_like(kp_scr)
        vp_scr[...] = jnp.zeros_like(vp_scr)

    qi = lax.broadcasted_iota(jnp.int32, (L, 2 * L), 0)
    ki = lax.broadcasted_iota(jnp.int32, (L, 2 * L), 1)
    dist = qi + L - ki
    in_band = (dist >= 0) & (dist <= L)
    lane_q = lax.broadcasted_iota(jnp.int32, (L, LANES), 1)
    lane_v = lax.broadcasted_iota(jnp.int32, (2 * L, LANES), 1)
    zero = jnp.zeros((), BF16)
    heads_per_pair = LANES // ATTN_HEAD_DIM

    def block(r, u):
        valid = in_band & (ki + (n * qblocks + u - 1) * L >= 0)
        lse_tile = jnp.zeros((L, LANES), F32)
        o_parts = []
        for j in range(width // LANES):
            sl = slice(j * LANES, (j + 1) * LANES)
            qp = q_ref[r, u * L:(u + 1) * L, sl]
            if u == 0:
                kk = jnp.concatenate([kp_scr[r, :, sl], k_ref[r, 0:L, sl]], axis=0)
                vv = jnp.concatenate([vp_scr[r, :, sl], v_ref[r, 0:L, sl]], axis=0)
            else:
                kk = k_ref[r, (u - 1) * L:(u + 1) * L, sl]
                vv = v_ref[r, (u - 1) * L:(u + 1) * L, sl]
            acc = jnp.zeros((L, LANES), F32)
            for hh in range(heads_per_pair):
                in_head_q = lane_q < ATTN_HEAD_DIM if hh == 0 else lane_q >= ATTN_HEAD_DIM
                in_head_v = lane_v < ATTN_HEAD_DIM if hh == 0 else lane_v >= ATTN_HEAD_DIM
                s = _dot_nt(jnp.where(in_head_q, qp, zero), kk)
                s = jnp.where(valid, s, MASK_VALUE)
                m = jnp.max(s, axis=-1, keepdims=True)
                p = jnp.exp(s - m)
                l = jnp.sum(p, axis=-1, keepdims=True)
                o = _dot(p.astype(BF16), jnp.where(in_head_v, vv, zero))
                acc = acc + o / l
                lse_tile = jnp.where(lane_q == j * heads_per_pair + hh, m + jnp.log(l), lse_tile)
            o_parts.append(acc)
        return o_parts, lse_tile

    def residue(r, carry):
        for u in range(qblocks):
            o_parts, lse_tile = block(r, u)
            if dilation == 1:
                o_ref[u * L:(u + 1) * L, :] = jnp.concatenate(o_parts, axis=1).astype(o_ref.dtype)
                lse_ref[u * L:(u + 1) * L, :] = lse_tile
            else:
                o_scr, lse_scr = tile_scr
                rows = pl.ds(u * L * dilation + r, L, stride=dilation)
                for j, part in enumerate(o_parts):
                    o_scr[j, rows, :] = part
                lse_scr[rows, :] = lse_tile
        kp_scr[r] = k_ref[r, (qblocks - 1) * L:qblocks * L, :]
        vp_scr[r] = v_ref[r, (qblocks - 1) * L:qblocks * L, :]
        return carry

    if dilation == 1:
        residue(0, 0)
    else:
        lax.fori_loop(0, dilation, residue, 0, unroll=2)
        o_scr, lse_scr = tile_scr
        for j in range(width // LANES):
            o_ref[:, j * LANES:(j + 1) * LANES] = o_scr[j].astype(o_ref.dtype)
        lse_ref[...] = lse_scr[...]


def _attn_group(qkv_g, batch, seq, dilation, span):
    width = ATTN_HEADS * ATTN_HEAD_DIM
    m = seq // dilation
    qblocks = 2 if dilation == 1 else 1
    nsteps = m // (span * qblocks)
    t = batch * seq
    rows = span * qblocks * dilation
    spec = lambda which: pl.BlockSpec((None, dilation, span * qblocks, width), lambda b, n: (b, 0, n, which))
    scratch = [pltpu.VMEM((dilation, span, width), BF16), pltpu.VMEM((dilation, span, width), BF16)]
    if dilation > 1:
        scratch += [pltpu.VMEM((width // LANES, rows, LANES), F32), pltpu.VMEM((rows, LANES), F32)]
    return pl.pallas_call(
        functools.partial(_attn_kernel, dilation=dilation, span=span),
        grid=(batch, nsteps),
        in_specs=[spec(0), spec(1), spec(2)],
        out_specs=[pl.BlockSpec((rows, width), lambda b, n: (b * nsteps + n, 0)),
                   pl.BlockSpec((rows, LANES), lambda b, n: (b * nsteps + n, 0))],
        out_shape=[jax.ShapeDtypeStruct((t, width), BF16),
                   jax.ShapeDtypeStruct((t, LANES), F32)],
        scratch_shapes=scratch,
        compiler_params=_cparams("parallel", "arbitrary"),
        name=f"attn_d{dilation}",
    )(qkv_g, qkv_g, qkv_g)


def _attn_out_kernel(o0_ref, o1_ref, o2_ref, l0_ref, l1_ref, l2_ref, w_ref, x_ref, gate_ref, gp_ref,
                     out_ref):
    tm, width = o0_ref.shape
    lane = lax.broadcasted_iota(jnp.int32, (tm, LANES), 1)
    first_head = lane < ATTN_HEAD_DIM
    lses = (l0_ref[...], l1_ref[...], l2_ref[...])
    outs = (o0_ref, o1_ref, o2_ref)
    parts = []
    for j in range(width // LANES):
        sl = slice(j * LANES, (j + 1) * LANES)
        ls = [jnp.where(first_head, l[:, 2 * j:2 * j + 1], l[:, 2 * j + 1:2 * j + 2]) for l in lses]
        mx = jnp.maximum(jnp.maximum(ls[0], ls[1]), ls[2])
        es = [jnp.exp(l - mx) for l in ls]
        num = sum(e * o[:, sl].astype(F32) for e, o in zip(es, outs))
        parts.append(num / (es[0] + es[1] + es[2]))
    o = jnp.concatenate(parts, axis=1).astype(BF16)
    out_ref[...] = _residual(x_ref[...], _dot(o, w_ref[...]), gate_ref[...], gp_ref[...])


def _attn_out(outs, lses, w, x, seq, modl, which, g_post):
    t, d = x.shape
    width = w.shape[0]
    tm = ROW_TILE
    per = seq // tm
    row = lambda n: pl.BlockSpec((tm, n), lambda i: (i, 0))
    return pl.pallas_call(
        _attn_out_kernel,
        grid=(t // tm,),
        in_specs=[row(width)] * 3 + [row(LANES)] * 3 + [
            _resident((width, d)),
            row(d),
            pl.BlockSpec((None, 1, d), lambda i: ((i // per) * N_MOD + which, 0, 0)),
            pl.BlockSpec((1, d), lambda i: (0, 0))],
        out_specs=row(d),
        out_shape=jax.ShapeDtypeStruct((t, d), F32),
        compiler_params=_cparams("parallel"),
        name="attn_out",
    )(*outs, *lses, w, x, modl, g_post.reshape(1, d))


def _rope_tables(positions):
    half = ROPE_DIM // 2
    inv_freq = ROPE_THETA ** (-jnp.arange(0, ROPE_DIM, 2, dtype=F32) / ROPE_DIM)
    ang = positions.astype(F32).reshape(-1, 1) * inv_freq
    cos, sin = jnp.cos(ang), jnp.sin(ang)
    dim = jnp.arange(LANES) % ATTN_HEAD_DIM
    freq = jnp.arange(half)[:, None]
    spread = lambda v, lanes: jnp.dot(v, ((dim[None, :] % half == freq) & lanes[None, :]).astype(F32),
                                      precision=lax.Precision.HIGHEST)
    c = spread(cos, dim < ROPE_DIM) + (dim >= ROPE_DIM).astype(F32)
    s1 = spread(-sin, dim < half)
    s2 = spread(sin, (dim >= half) & (dim < ROPE_DIM))
    return c, s1, s2


def _attention_layer(xf, batch, seq, g_pre, g_post, modl, rope, w_qkv, w_out):
    dilations = [d for _, d in ATTN_CONFIGS]
    qkvs = _inproj_qkv(xf, seq, g_pre, modl, 0, w_qkv, rope, dilations, ATTN_HEAD_DIM ** -0.5)
    outs, lses = [], []
    for qkv_g, (window, dilation) in zip(qkvs, ATTN_CONFIGS):
        qkv_g = qkv_g.reshape(batch, dilation, seq // dilation, qkv_g.shape[-1])
        o, lse = _attn_group(qkv_g, batch, seq, dilation, window // dilation)
        outs.append(o)
        lses.append(lse)
    return _attn_out(outs, lses, w_out, xf, seq, modl, 2, g_post)


def kernel(x, c, positions, ada_w, ada_b, norm_mix_pre, norm_mix_post, norm_mlp_pre, norm_mlp_post, mlp_w1, mlp_w2, ssd_w_in, ssd_conv_w, ssd_conv_b, ssd_dt_bias, ssd_a_log, ssd_d, ssd_norm, ssd_w_out, lru_w_in, lru_conv_w, lru_conv_b, lru_w_gate_a, lru_b_gate_a, lru_w_gate_x, lru_b_gate_x, lru_lambda, lru_w_out, attn_w_qkv, attn_w_out):
    batch, seq, d = x.shape
    depth = ada_w.shape[0]
    xf = x.reshape(batch * seq, d)
    mod = _modulation(c, ada_w, ada_b)
    rope = _rope_tables(positions)
    w1_all, w2_all = mlp_w1.astype(BF16), mlp_w2.astype(BF16)
    for layer in range(depth):
        modl = mod[layer].reshape(batch * N_MOD, 1, d)
        kind, occ = layer % N_MIXERS, layer // N_MIXERS
        if kind == 0:
            n_heads = ssd_dt_bias.shape[1]
            d_inner = n_heads * SSD_HEAD_DIM
            n_main = ssd_w_in.shape[2] - n_heads
            w_dt = jnp.pad(ssd_w_in[occ, :, n_main:], ((0, 0), (0, LANES - n_heads))).astype(BF16)
            z, xbc, dt_raw = _inproj(xf, seq, norm_mix_pre[layer], modl, 0,
                                     ssd_w_in[occ, :, :n_main].astype(BF16), d_inner, w_dt=w_dt)
            y = _ssd_core(xbc, dt_raw, batch, seq, ssd_conv_w[occ], ssd_conv_b[occ], ssd_dt_bias[occ],
                          ssd_a_log[occ], ssd_d[occ])
            xf = _outproj(y, z, ssd_w_out[occ].astype(BF16), xf, seq, modl, 2, norm_mix_post[layer],
                          norm_g=ssd_norm[occ], n_groups=SSD_N_GROUPS)
        elif kind == 1:
            width = lru_conv_w.shape[-1]
            gate, xc = _inproj(xf, seq, norm_mix_pre[layer], modl, 0, lru_w_in[occ].astype(BF16), width,
                               conv_w=lru_conv_w[occ], conv_b=lru_conv_b[occ])
            hs = _lru_core(xc, batch, seq, lru_w_gate_a[occ], lru_b_gate_a[occ], lru_w_gate_x[occ],
                           lru_b_gate_x[occ], lru_lambda[occ])
            xf = _outproj(hs, gate, lru_w_out[occ].astype(BF16), xf, seq, modl, 2, norm_mix_post[layer])
        else:
            xf = _attention_layer(xf, batch, seq, norm_mix_pre[layer], norm_mix_post[layer], modl, rope,
                                  attn_w_qkv[occ].astype(BF16), attn_w_out[occ].astype(BF16))
        xf = _mlp(xf, seq, modl, norm_mlp_pre[layer], norm_mlp_post[layer], w1_all, w2_all, layer)
    return xf.reshape(batch, seq, d)
```
